```python
import math
import jax, jax.numpy as jnp
from jax import lax
import numpy as np

D_MODEL = 4096
BATCH = 2
SEQ = 8192
DEPTH = 4
DEC_BATCH = 2
DEC_SEQ = 4096
PAST_LEN = 128

N_MIXERS = 2
N_ATTN_LAYERS = (DEPTH + 1) // 2
N_RWKV_LAYERS = DEPTH // 2
HEAD_DIM = 128
N_HEADS = D_MODEL // HEAD_DIM
N_KV_HEADS = N_HEADS // 4
GQA_GROUP = N_HEADS // N_KV_HEADS
Q_DIM = N_HEADS * HEAD_DIM
KV_DIM = N_KV_HEADS * HEAD_DIM
QKV_DIM = Q_DIM + 2 * KV_DIM
WINDOW = 128
BLOCK = 128
ATTN_SCALE = 1.0 / math.sqrt(HEAD_DIM)
NEG_INF = -1e30
N_BUCKETS = 32
MAX_DISTANCE = 128
RWKV_HEAD_DIM = 64
RWKV_HEADS = D_MODEL // RWKV_HEAD_DIM
LORA_DECAY = max(32, int(round(1.8 * D_MODEL ** 0.5 / 32)) * 32)
LORA_AAA = max(32, int(round(1.8 * D_MODEL ** 0.5 / 32)) * 32)
LORA_GATE = max(32, int(round(0.6 * D_MODEL ** 0.8 / 32)) * 32)
GN_EPS = 64e-5
N_EXPERTS = 16
D_EXPERT = D_MODEL // 2
CAPACITY_FACTOR = 2
EPS = 1e-6

kernel_name = "hybrid_bidir_swa_rwkv7_ecmoe"


def rms_norm(x, g):
    xf = x.astype(jnp.float32)
    y = xf * lax.rsqrt(jnp.mean(xf * xf, axis=-1, keepdims=True) + EPS)
    return (y * g.astype(jnp.float32)).astype(x.dtype)


def t5_bucket(rel):
    nb = N_BUCKETS // 2
    max_exact = nb // 2
    ret = jnp.where(rel > 0, nb, 0)
    n = jnp.abs(rel)
    nf = jnp.maximum(n, 1).astype(jnp.float32)
    large = max_exact + (jnp.log(nf / max_exact) / math.log(MAX_DISTANCE / max_exact)
                         * (nb - max_exact)).astype(jnp.int32)
    large = jnp.minimum(large, nb - 1)
    return ret + jnp.where(n < max_exact, n, large)


def _band(t, nb):
    B = t.shape[0]
    tp = jnp.pad(t, ((0, 0), (BLOCK, BLOCK), (0, 0), (0, 0))).reshape(B, nb + 2, BLOCK, *t.shape[2:])
    return jnp.concatenate([tp[:, :-2], tp[:, 1:-1], tp[:, 2:]], axis=2)


def _band_bias_and_mask(rel_bias, nb, seq):
    q_pos = jnp.arange(BLOCK)[:, None]
    k_off = jnp.arange(3 * BLOCK)[None, :] - BLOCK
    rel = k_off - q_pos
    bias = rel_bias.astype(jnp.float32)[t5_bucket(rel)]
    bias = jnp.transpose(bias, (2, 0, 1)).reshape(N_KV_HEADS, GQA_GROUP, BLOCK, 3 * BLOCK)
    kpos = jnp.arange(nb)[:, None] * BLOCK + k_off
    mask = (jnp.abs(rel) <= WINDOW)[None] & ((kpos >= 0) & (kpos < seq))[:, None, :]
    return bias, mask


def windowed_gqa(h, w_qkv, q_gain, k_gain, sink, w_o, rel_bias):
    B, S, _ = h.shape
    nb = S // BLOCK
    qkv = h @ w_qkv
    q = qkv[..., :Q_DIM].reshape(B, S, N_HEADS, HEAD_DIM)
    k = qkv[..., Q_DIM:Q_DIM + KV_DIM].reshape(B, S, N_KV_HEADS, HEAD_DIM)
    v = qkv[..., Q_DIM + KV_DIM:].reshape(B, S, N_KV_HEADS, HEAD_DIM)
    q = rms_norm(q, q_gain).reshape(B, nb, BLOCK, N_KV_HEADS, GQA_GROUP, HEAD_DIM)
    k = rms_norm(k, k_gain)
    kb, vb = _band(k, nb), _band(v, nb)
    bias, mask = _band_bias_and_mask(rel_bias, nb, S)
    logits = jnp.einsum("bnqhgd,bnkhd->bnhgqk", q, kb).astype(jnp.float32) * ATTN_SCALE
    logits = jnp.where(mask[None, :, None, None], logits + bias[None, None], NEG_INF)
    sink_col = jnp.broadcast_to(sink.astype(jnp.float32).reshape(N_KV_HEADS, GQA_GROUP, 1, 1),
                                logits.shape[:-1] + (1,))
    probs = jax.nn.softmax(jnp.concatenate([logits, sink_col], axis=-1), axis=-1)[..., :-1]
    out = jnp.einsum("bnhgqk,bnkhd->bnqhgd", probs.astype(v.dtype), vb).reshape(B, S, Q_DIM)
    return out @ w_o


def wkv7_scan(r, w, k, v, kk, a, reverse):
    B, S, H, N = r.shape

    def step(state, inp):
        r_t, w_t, k_t, v_t, kk_t, a_t = inp
        sa = jnp.einsum("bhvk,bhk->bhv", state, -kk_t)
        state = (state * w_t[:, :, None, :]
                 + sa[..., None] * (kk_t * a_t)[:, :, None, :]
                 + v_t[..., None] * k_t[:, :, None, :])
        return state, jnp.einsum("bhvk,bhk->bhv", state, r_t)

    xs = tuple(jnp.moveaxis(t, 1, 0) for t in (r, w, k, v, kk, a))
    _, ys = lax.scan(step, jnp.zeros((B, H, N, N), jnp.float32), xs, reverse=reverse)
    return jnp.moveaxis(ys, 0, 1)


def rwkv7_bidirectional(h, mu, w_rkv, w0, w1, w2, a0, a1, a2, g1, g2, k_k, k_a, r_k, ln_w, ln_b, w_o):
    B, S, D = h.shape
    f32 = jnp.float32
    heads = lambda t: t.reshape(B, S, RWKV_HEADS, RWKV_HEAD_DIM)
    x_prev = jnp.pad(h, ((0, 0), (1, 0), (0, 0)))[:, :-1]
    x_next = jnp.pad(h, ((0, 0), (0, 1), (0, 0)))[:, 1:]
    xx = 0.5 * (x_prev + x_next) - h
    xr, xw, xk, xv, xa, xg = [h + xx * mu[c] for c in range(6)]
    r = heads((xr @ w_rkv[0]).astype(f32))
    k = (xk @ w_rkv[1]).astype(f32)
    v = heads((xv @ w_rkv[2]).astype(f32))
    g = jax.nn.sigmoid(xg @ g1) @ g2
    kk = heads(k * k_k.astype(f32))
    kk = kk / jnp.maximum(jnp.sqrt(jnp.sum(kk * kk, axis=-1, keepdims=True)), 1e-12)
    ys, bonus = [], []
    for d, rev in enumerate((False, True)):
        w_log = -jax.nn.softplus(-(w0[d] + jnp.tanh(xw @ w1[d]) @ w2[d]).astype(f32)) - 0.5
        decay = jnp.exp(-jnp.exp(w_log))
        a = jax.nn.sigmoid((a0[d] + (xa @ a1[d]) @ a2[d]).astype(f32))
        k_d = heads(k * (1.0 + (a - 1.0) * k_a.astype(f32)))
        ys.append(wkv7_scan(r, heads(decay), k_d, v, kk, heads(a), rev))
        bonus.append(jnp.sum(r * k_d * r_k.astype(f32), axis=-1, keepdims=True) * v)
    y = ys[0] + ys[1]
    mean = jnp.mean(y, axis=-1, keepdims=True)
    var = jnp.mean(jnp.square(y - mean), axis=-1, keepdims=True)
    yn = ((y - mean) * lax.rsqrt(var + GN_EPS)).reshape(B, S, D) * ln_w.astype(f32) + ln_b.astype(f32)
    out = (yn + (bonus[0] + bonus[1]).reshape(B, S, D)) * g.astype(f32)
    return out.astype(h.dtype) @ w_o


def expert_choice_ffn(h, w_router, w_gate, w_up, w_down):
    B, S, D = h.shape
    n = B * S
    cap = CAPACITY_FACTOR * n // N_EXPERTS
    xt = h.reshape(n, D)
    aff = jax.nn.softmax((xt @ w_router).astype(jnp.float32), axis=-1)
    gate, idx = lax.top_k(aff.T, cap)
    xe = xt[idx]
    hid = jax.nn.silu(jnp.einsum("ecd,edf->ecf", xe, w_gate)) * jnp.einsum("ecd,edf->ecf", xe, w_up)
    ye = jnp.einsum("ecf,efd->ecd", hid, w_down) * gate[..., None].astype(xe.dtype)
    out = jnp.zeros_like(xt).at[idx.reshape(-1)].add(ye.reshape(-1, D))
    return out.reshape(B, S, D)


def setup_inputs(seed: int = 0) -> dict:
    key = jax.random.key(seed)
    ks = jax.random.split(key, 32)
    f32 = jnp.float32
    D = D_MODEL
    NA, NR = N_ATTN_LAYERS, N_RWKV_LAYERS

    def nrm(k, shape, scale):
        return jax.random.normal(k, shape, f32) * scale

    return {
        "x_prompt": nrm(ks[0], (BATCH, SEQ, D), 1.0),
        "x_sample": nrm(ks[1], (DEC_BATCH, DEC_SEQ, D), 1.0),
        "rel_bias": nrm(ks[2], (N_BUCKETS, N_HEADS), 0.5),
        "norm_mix": 1.0 + nrm(ks[3], (DEPTH, D), 0.02),
        "norm_ffn": 1.0 + nrm(ks[4], (DEPTH, D), 0.02),
        "attn_w_qkv": nrm(ks[5], (NA, D, QKV_DIM), D ** -0.5),
        "attn_q_gain": 1.0 + nrm(ks[6], (NA, HEAD_DIM), 0.02),
        "attn_k_gain": 1.0 + nrm(ks[7], (NA, HEAD_DIM), 0.02),
        "attn_sink": nrm(ks[8], (NA, N_HEADS), 0.5),
        "attn_w_o": nrm(ks[9], (NA, Q_DIM, D), 0.5 * Q_DIM ** -0.5),
        "rwkv_mu": jax.random.uniform(ks[10], (NR, 6, D), f32),
        "rwkv_w_rkv": nrm(ks[11], (NR, 3, D, D), D ** -0.5),
        "rwkv_w0": jax.random.uniform(ks[12], (NR, 2, D), f32, -6.0, 0.0),
        "rwkv_w1": nrm(ks[13], (NR, 2, D, LORA_DECAY), D ** -0.5),
        "rwkv_w2": nrm(ks[14], (NR, 2, LORA_DECAY, D), 0.1 * LORA_DECAY ** -0.5),
        "rwkv_a0": nrm(ks[15], (NR, 2, D), 0.1),
        "rwkv_a1": nrm(ks[16], (NR, 2, D, LORA_AAA), D ** -0.5),
        "rwkv_a2": nrm(ks[17], (NR, 2, LORA_AAA, D), LORA_AAA ** -0.5),
        "rwkv_g1": nrm(ks[18], (NR, D, LORA_GATE), D ** -0.5),
        "rwkv_g2": nrm(ks[19], (NR, LORA_GATE, D), LORA_GATE ** -0.5),
        "rwkv_k_k": 0.85 + nrm(ks[20], (NR, D), 0.05),
        "rwkv_k_a": 1.0 + nrm(ks[21], (NR, D), 0.05),
        "rwkv_r_k": nrm(ks[22], (NR, RWKV_HEADS, RWKV_HEAD_DIM), 0.1),
        "rwkv_ln_w": 1.0 + nrm(ks[23], (NR, D), 0.02),
        "rwkv_ln_b": nrm(ks[24], (NR, D), 0.01),
        "rwkv_w_o": nrm(ks[25], (NR, D, D), 0.5 * D ** -0.5),
        "moe_router": nrm(ks[26], (DEPTH, D, N_EXPERTS), D ** -0.5),
        "moe_w_gate": nrm(ks[27], (DEPTH, N_EXPERTS, D, D_EXPERT), D ** -0.5),
        "moe_w_up": nrm(ks[28], (DEPTH, N_EXPERTS, D, D_EXPERT), D ** -0.5),
        "moe_w_down": nrm(ks[29], (DEPTH, N_EXPERTS, D_EXPERT, D), D_EXPERT ** -0.5),
    }


def reference(x_prompt, x_sample, rel_bias, norm_mix, norm_ffn, attn_w_qkv, attn_q_gain, attn_k_gain,
              attn_sink, attn_w_o, rwkv_mu, rwkv_w_rkv, rwkv_w0, rwkv_w1, rwkv_w2, rwkv_a0, rwkv_a1,
              rwkv_a2, rwkv_g1, rwkv_g2, rwkv_k_k, rwkv_k_a, rwkv_r_k, rwkv_ln_w, rwkv_ln_b, rwkv_w_o,
              moe_router, moe_w_gate, moe_w_up, moe_w_down):
    def trunk(x):
        for i in range(DEPTH):
            j = i // N_MIXERS
            h = rms_norm(x, norm_mix[i])
            if i % N_MIXERS == 0:
                mixed = windowed_gqa(h, attn_w_qkv[j], attn_q_gain[j], attn_k_gain[j], attn_sink[j],
                                     attn_w_o[j], rel_bias)
            else:
                mixed = rwkv7_bidirectional(h, rwkv_mu[j], rwkv_w_rkv[j], rwkv_w0[j], rwkv_w1[j], rwkv_w2[j],
                                            rwkv_a0[j], rwkv_a1[j], rwkv_a2[j], rwkv_g1[j], rwkv_g2[j],
                                            rwkv_k_k[j], rwkv_k_a[j], rwkv_r_k[j], rwkv_ln_w[j],
                                            rwkv_ln_b[j], rwkv_w_o[j])
            x = x + mixed
            x = x + expert_choice_ffn(rms_norm(x, norm_ffn[i]), moe_router[i], moe_w_gate[i],
                                      moe_w_up[i], moe_w_down[i])
        return x

    y_prompt = trunk(x_prompt)
    y_sample = trunk(x_sample)
    return (y_prompt, y_sample)
```

```python
import functools
import math

import jax
import jax.numpy as jnp
from jax import lax
from jax.experimental import pallas as pl
from jax.experimental.pallas import tpu as pltpu

F32 = jnp.float32
BF16 = jnp.bfloat16

D_MODEL = 4096
DEPTH = 4
HEAD_DIM = 128
N_HEADS = D_MODEL // HEAD_DIM
N_KV_HEADS = N_HEADS // 4
GQA_GROUP = N_HEADS // N_KV_HEADS
Q_DIM = N_HEADS * HEAD_DIM
KV_DIM = N_KV_HEADS * HEAD_DIM
WINDOW = 128
BLOCK = 128
ATTN_SCALE = 1.0 / math.sqrt(HEAD_DIM)
NEG_INF = -1e30
N_BUCKETS = 32
MAX_DISTANCE = 128
RWKV_HEAD_DIM = 64
RWKV_HEADS = D_MODEL // RWKV_HEAD_DIM
GN_EPS = 64e-5
N_EXPERTS = 16
D_EXPERT = D_MODEL // 2
CAPACITY_FACTOR = 2
EPS = 1e-6

V7X_VMEM_LIMIT_BYTES = 56 * 1024 * 1024


def _params(*sem):
    return pltpu.CompilerParams(dimension_semantics=sem, vmem_limit_bytes=V7X_VMEM_LIMIT_BYTES)


def _mm_kernel(x_ref, w_ref, o_ref):
    o_ref[...] = jnp.dot(x_ref[...], w_ref[...].astype(BF16),
                         preferred_element_type=F32).astype(o_ref.dtype)


def _matmul(x, w, widx=(), *, out_dtype=F32, tm=1024, tn=512):
    M, K = x.shape
    N = w.shape[-1]
    tm = min(tm, M)
    tn = min(tn, N)
    assert M % tm == 0 and N % tn == 0 and w.shape[-2] == K
    nlead = len(widx)
    assert w.ndim == nlead + 2
    w_block = (None,) * nlead + (K, tn)
    return pl.pallas_call(
        _mm_kernel,
        grid=(M // tm, N // tn),
        in_specs=[pl.BlockSpec((tm, K), lambda i, j: (i, 0)),
                  pl.BlockSpec(w_block, lambda i, j: tuple(widx) + (0, j))],
        out_specs=pl.BlockSpec((tm, tn), lambda i, j: (i, j)),
        out_shape=jax.ShapeDtypeStruct((M, N), out_dtype),
        compiler_params=_params("parallel", "arbitrary"),
        name="dense_matmul",
    )(x, w)


def _t5_bucket(rel):
    nb = N_BUCKETS // 2
    max_exact = nb // 2
    ret = jnp.where(rel > 0, nb, 0)
    n = jnp.abs(rel)
    nf = jnp.maximum(n, 1).astype(F32)
    large = max_exact + (jnp.log(nf / max_exact) / math.log(MAX_DISTANCE / max_exact)
                         * (nb - max_exact)).astype(jnp.int32)
    large = jnp.minimum(large, nb - 1)
    return ret + jnp.where(n < max_exact, n, large)


def _band_bias(rel_bias):
    q_pos = jnp.arange(BLOCK)[:, None]
    k_off = jnp.arange(3 * BLOCK)[None, :] - BLOCK
    bias = rel_bias.astype(F32)[_t5_bucket(k_off - q_pos)]
    return jnp.transpose(bias, (2, 0, 1))


def _rms(x, g):
    return x * lax.rsqrt(jnp.mean(x * x, axis=-1, keepdims=True) + EPS) * g


def _attn_kernel(sink_ref, q_ref, kp_ref, kc_ref, kn_ref, vp_ref, vc_ref, vn_ref, bias_ref,
                 qg_ref, kg_ref, o_ref, *, nb):
    h = pl.program_id(0)
    n = pl.program_id(2)
    k = jnp.concatenate([kp_ref[...], kc_ref[...], kn_ref[...]], axis=0)
    k = _rms(k, kg_ref[...]).astype(BF16)
    v = jnp.concatenate([vp_ref[...], vc_ref[...], vn_ref[...]], axis=0).astype(BF16)
    row = lax.broadcasted_iota(jnp.int32, (BLOCK, 3 * BLOCK), 0)
    col = lax.broadcasted_iota(jnp.int32, (BLOCK, 3 * BLOCK), 1)
    rel = col - BLOCK - row
    valid = (jnp.abs(rel) <= WINDOW)
    valid = valid & ((col >= BLOCK) | (n > 0)) & ((col < 2 * BLOCK) | (n < nb - 1))
    for g in range(GQA_GROUP):
        q = _rms(q_ref[:, g * HEAD_DIM:(g + 1) * HEAD_DIM], qg_ref[...]).astype(BF16)
        logits = lax.dot_general(q, k, (((1,), (1,)), ((), ())), preferred_element_type=F32)
        logits = jnp.where(valid, logits * ATTN_SCALE + bias_ref[g], NEG_INF)
        sink = sink_ref[h * GQA_GROUP + g]
        m = jnp.maximum(jnp.max(logits, axis=-1, keepdims=True), sink)
        p = jnp.exp(logits - m)
        den = jnp.sum(p, axis=-1, keepdims=True) + jnp.exp(sink - m)
        probs = (p / den).astype(BF16)
        o_ref[:, g * HEAD_DIM:(g + 1) * HEAD_DIM] = jnp.dot(
            probs, v, preferred_element_type=F32).astype(o_ref.dtype)


def _attention(qkv, bias, sink, q_gain, k_gain):
    B, S, _ = qkv.shape
    nb = S // BLOCK
    kcol = Q_DIM // HEAD_DIM
    vcol = (Q_DIM + KV_DIM) // HEAD_DIM

    def band(col0, shift):
        def index(h, b, n):
            return (b, jnp.clip(n + shift, 0, nb - 1), col0 + h)
        return pl.BlockSpec((None, BLOCK, HEAD_DIM), index)

    qw = GQA_GROUP * HEAD_DIM
    return pl.pallas_call(
        functools.partial(_attn_kernel, nb=nb),
        grid=(N_KV_HEADS, B, nb),
        in_specs=[pl.BlockSpec(memory_space=pltpu.SMEM),
                  pl.BlockSpec((None, BLOCK, qw), lambda h, b, n: (b, n, h)),
                  band(kcol, -1), band(kcol, 0), band(kcol, 1),
                  band(vcol, -1), band(vcol, 0), band(vcol, 1),
                  pl.BlockSpec((GQA_GROUP, BLOCK, 3 * BLOCK), lambda h, b, n: (h, 0, 0)),
                  pl.BlockSpec((1, HEAD_DIM), lambda h, b, n: (0, 0)),
                  pl.BlockSpec((1, HEAD_DIM), lambda h, b, n: (0, 0))],
        out_specs=pl.BlockSpec((None, BLOCK, qw), lambda h, b, n: (b, n, h)),
        out_shape=jax.ShapeDtypeStruct((B, S, Q_DIM), BF16),
        compiler_params=_params("parallel", "parallel", "arbitrary"),
        name="windowed_gqa",
    )(sink, qkv, qkv, qkv, qkv, qkv, qkv, qkv, bias, q_gain.reshape(1, -1), k_gain.reshape(1, -1))


WKV_ROWS = 4
WKV_TCHUNK = 64


def _wkv_kernel(kkn_ref, v_ref, wr_ref, w_ref, b_ref, kd_ref, br_ref, kr_ref, y_ref, s_ref):
    d = pl.program_id(0)
    tc = kkn_ref.shape[0]
    nv = s_ref.shape[0]

    @pl.when(pl.program_id(1) == 0)
    def _():
        s_ref[...] = jnp.zeros_like(s_ref)

    def group(gi, carry):
        v0 = gi * WKV_ROWS
        state = tuple(s_ref[v0 + r] for r in range(WKV_ROWS))

        def step(tt, state):
            t = jnp.where(d == 0, tt, tc - 1 - tt)
            kkn = kkn_ref[t]
            wr = wr_ref[t]
            w = w_ref[t]
            b = b_ref[t]
            kd = kd_ref[t]
            br = br_ref[pl.ds(t, 1), :]
            kr = kr_ref[pl.ds(t, 1), :]
            new = []
            for r in range(WKV_ROWS):
                s = state[r]
                sa = jnp.sum(s * kkn, axis=0, keepdims=True)
                p = jnp.sum(s * wr, axis=0, keepdims=True)
                vv = v_ref[t, pl.ds(v0 + r, 1), :]
                new.append(s * w + sa * b + vv * kd)
                y_ref[t, pl.ds(v0 + r, 1), :] = p + sa * br + vv * kr
            return tuple(new)

        state = lax.fori_loop(0, tc, step, state)
        for r in range(WKV_ROWS):
            s_ref[v0 + r] = state[r]
        return carry

    lax.fori_loop(0, nv // WKV_ROWS, group, 0)


def _wkv_scan(kkn, v, wr, w, b, kd, br, kr):
    S, N, C = kkn.shape
    tc = WKV_TCHUNK
    nt = S // tc

    def tmap(d, i):
        return i + d * (nt - 1 - 2 * i)

    shared = pl.BlockSpec((tc, N, C), lambda d, i: (tmap(d, i), 0, 0))
    per_dir = pl.BlockSpec((None, tc, N, C), lambda d, i: (d, tmap(d, i), 0, 0))
    scal = pl.BlockSpec((None, tc, C), lambda d, i: (d, tmap(d, i), 0))
    return pl.pallas_call(
        _wkv_kernel,
        grid=(2, nt),
        in_specs=[shared, shared, per_dir, per_dir, per_dir, per_dir, scal, scal],
        out_specs=per_dir,
        out_shape=jax.ShapeDtypeStruct((2, S, N, C), F32),
        scratch_shapes=[pltpu.VMEM((N, N, C), F32)],
        compiler_params=_params("arbitrary", "arbitrary"),
        name="wkv7_scan",
    )(kkn, v, wr, w, b, kd, br, kr)


def _gate_up_kernel(x_ref, wg_ref, wu_ref, o_ref):
    x = x_ref[...]
    g = jnp.dot(x, wg_ref[...].astype(BF16), preferred_element_type=F32)
    u = jnp.dot(x, wu_ref[...].astype(BF16), preferred_element_type=F32)
    o_ref[...] = (g * jax.nn.sigmoid(g) * u).astype(o_ref.dtype)


def _down_kernel(h_ref, wd_ref, gate_ref, o_ref):
    y = jnp.dot(h_ref[...], wd_ref[...].astype(BF16), preferred_element_type=F32)
    o_ref[...] = y * gate_ref[...]


def _expert_ffn(xe, gate, w_gate, w_up, w_down, layer, *, tm=1024, tf=256, tn=1024):
    E, C, D = xe.shape
    Fd = w_gate.shape[-1]
    tm = min(tm, C)
    hid = pl.pallas_call(
        _gate_up_kernel,
        grid=(E, C // tm, Fd // tf),
        in_specs=[pl.BlockSpec((None, tm, D), lambda e, i, j: (e, i, 0)),
                  pl.BlockSpec((None, None, D, tf), lambda e, i, j: (layer, e, 0, j)),
                  pl.BlockSpec((None, None, D, tf), lambda e, i, j: (layer, e, 0, j))],
        out_specs=pl.BlockSpec((None, tm, tf), lambda e, i, j: (e, i, j)),
        out_shape=jax.ShapeDtypeStruct((E, C, Fd), BF16),
        compiler_params=_params("parallel", "parallel", "arbitrary"),
        name="expert_gate_up",
    )(xe, w_gate, w_up)
    return pl.pallas_call(
        _down_kernel,
        grid=(E, C // tm, D // tn),
        in_specs=[pl.BlockSpec((None, tm, Fd), lambda e, i, j: (e, i, 0)),
                  pl.BlockSpec((None, None, Fd, tn), lambda e, i, j: (layer, e, 0, j)),
                  pl.BlockSpec((None, tm, 1), lambda e, i, j: (e, i, 0))],
        out_specs=pl.BlockSpec((None, tm, tn), lambda e, i, j: (e, i, j)),
        out_shape=jax.ShapeDtypeStruct((E, C, D), F32),
        compiler_params=_params("parallel", "parallel", "arbitrary"),
        name="expert_down",
    )(hid, w_down, gate)


def _rms_norm(x, g):
    y = x * lax.rsqrt(jnp.mean(x * x, axis=-1, keepdims=True) + EPS)
    return y * g


def _attn_layer(h, w_qkv, j, q_gain, k_gain, sink, w_o, bias):
    B, S, D = h.shape
    qkv = _matmul(h.reshape(B * S, D).astype(BF16), w_qkv, (j,)).reshape(B, S, -1)
    o = _attention(qkv, bias, sink, q_gain, k_gain)
    return _matmul(o.reshape(B * S, Q_DIM), w_o, (j,)).reshape(B, S, D)


def _chains(t, B, S):
    t = t.reshape(B, S, RWKV_HEADS, RWKV_HEAD_DIM)
    return jnp.transpose(t, (1, 3, 0, 2)).reshape(S, RWKV_HEAD_DIM, B * RWKV_HEADS)


def _rwkv_layer(h, j, mu, w_rkv, w0, w1, w2, a0, a1, a2, g1, g2, k_k, k_a, r_k, ln_w, ln_b, w_o):
    B, S, D = h.shape
    n = B * S
    x_prev = jnp.pad(h, ((0, 0), (1, 0), (0, 0)))[:, :-1]
    x_next = jnp.pad(h, ((0, 0), (0, 1), (0, 0)))[:, 1:]
    xx = 0.5 * (x_prev + x_next) - h
    xr, xw, xk, xv, xa, xg = [(h + xx * mu[j, c]).reshape(n, D).astype(BF16) for c in range(6)]
    r = _matmul(xr, w_rkv, (j, 0))
    k = _matmul(xk, w_rkv, (j, 1))
    v = _matmul(xv, w_rkv, (j, 2))
    gpad = (-g1.shape[-1]) % 128
    g1p = jnp.pad(g1[j], ((0, 0), (0, gpad)))
    g2p = jnp.pad(g2[j], ((0, gpad), (0, 0)))
    g = _matmul(jax.nn.sigmoid(_matmul(xg, g1p)).astype(BF16), g2p)

    heads = lambda t: t.reshape(n, RWKV_HEADS, RWKV_HEAD_DIM)
    kk = heads(k * k_k[j])
    kk = kk / jnp.maximum(jnp.sqrt(jnp.sum(kk * kk, axis=-1, keepdims=True)), 1e-12)
    kk = kk.reshape(n, D)
    wr_l, w_l, b_l, kd_l, br_l, kr_l, bonus = [], [], [], [], [], [], 0.0
    for d in range(2):
        wl = _matmul(jnp.tanh(_matmul(xw, w1, (j, d))).astype(BF16), w2, (j, d))
        w_log = -jax.nn.softplus(-(w0[j, d] + wl)) - 0.5
        decay = jnp.exp(-jnp.exp(w_log))
        al = _matmul(_matmul(xa, a1, (j, d)).astype(BF16), a2, (j, d))
        a = jax.nn.sigmoid(a0[j, d] + al)
        k_d = k * (1.0 + (a - 1.0) * k_a[j])
        bvec = kk * a
        rk = heads(r * k_d)
        rb = heads(r * bvec)
        kr = jnp.sum(rk, axis=-1)
        br = jnp.sum(rb, axis=-1)
        bonus = bonus + jnp.sum(rk * r_k[j], axis=-1, keepdims=True) * heads(v)
        wr_l.append(_chains(decay * r, B, S))
        w_l.append(_chains(decay, B, S))
        b_l.append(_chains(bvec, B, S))
        kd_l.append(_chains(k_d, B, S))
        tr = lambda t: jnp.transpose(t.reshape(B, S, RWKV_HEADS), (1, 0, 2)).reshape(S, B * RWKV_HEADS)
        br_l.append(tr(br))
        kr_l.append(tr(kr))
    ys = _wkv_scan(_chains(-kk, B, S), _chains(v, B, S), jnp.stack(wr_l), jnp.stack(w_l),
                   jnp.stack(b_l), jnp.stack(kd_l), jnp.stack(br_l), jnp.stack(kr_l))
    y = ys[0] + ys[1]
    y = jnp.transpose(y.reshape(S, RWKV_HEAD_DIM, B, RWKV_HEADS), (2, 0, 3, 1))
    mean = jnp.mean(y, axis=-1, keepdims=True)
    var = jnp.mean(jnp.square(y - mean), axis=-1, keepdims=True)
    yn = ((y - mean) * lax.rsqrt(var + GN_EPS)).reshape(n, D) * ln_w[j] + ln_b[j]
    out = (yn + bonus.reshape(n, D)) * g
    return _matmul(out.astype(BF16), w_o, (j,)).reshape(B, S, D)


def _moe_layer(h, i, w_router, w_gate, w_up, w_down):
    B, S, D = h.shape
    n = B * S
    cap = CAPACITY_FACTOR * n // N_EXPERTS
    xt = h.reshape(n, D)
    logits = jnp.dot(xt, w_router[i], precision=lax.Precision.HIGHEST)
    aff = jax.nn.softmax(logits, axis=-1)
    gate, idx = lax.top_k(aff.T, cap)
    xe = xt.astype(BF16)[idx]
    ye = _expert_ffn(xe, gate[..., None], w_gate, w_up, w_down, i)
    out = jnp.zeros_like(xt).at[idx.reshape(-1)].add(ye.reshape(-1, D))
    return out.reshape(B, S, D)


def kernel(x_prompt, x_sample, rel_bias, norm_mix, norm_ffn, attn_w_qkv, attn_q_gain, attn_k_gain,
           attn_sink, attn_w_o, rwkv_mu, rwkv_w_rkv, rwkv_w0, rwkv_w1, rwkv_w2, rwkv_a0, rwkv_a1,
           rwkv_a2, rwkv_g1, rwkv_g2, rwkv_k_k, rwkv_k_a, rwkv_r_k, rwkv_ln_w, rwkv_ln_b, rwkv_w_o,
           moe_router, moe_w_gate, moe_w_up, moe_w_down):
    bias = _band_bias(rel_bias)

    def trunk(x):
        for i in range(DEPTH):
            j = i // 2
            h = _rms_norm(x, norm_mix[i])
            if i % 2 == 0:
                mixed = _attn_layer(h, attn_w_qkv, j, attn_q_gain[j], attn_k_gain[j], attn_sink[j],
                                    attn_w_o, bias)
            else:
                mixed = _rwkv_layer(h, j, rwkv_mu, rwkv_w_rkv, rwkv_w0, rwkv_w1, rwkv_w2, rwkv_a0,
                                    rwkv_a1, rwkv_a2, rwkv_g1, rwkv_g2, rwkv_k_k, rwkv_k_a, rwkv_r_k,
                                    rwkv_ln_w, rwkv_ln_b, rwkv_w_o)
            x = x + mixed
            x = x + _moe_layer(_rms_norm(x, norm_ffn[i]), i, moe_router, moe_w_gate, moe_w_up,
                               moe_w_down)
        return x

    return (trunk(x_prompt), trunk(x_sample))
```

```python
import functools
import math

import jax
import jax.numpy as jnp
from jax import lax
from jax.experimental import pallas as pl
from jax.experimental.pallas import tpu as pltpu

F32 = jnp.float32
BF16 = jnp.bfloat16

D_MODEL = 4096
DEPTH = 4
HEAD_DIM = 128
N_HEADS = D_MODEL // HEAD_DIM
N_KV_HEADS = N_HEADS // 4
GQA_GROUP = N_HEADS // N_KV_HEADS
Q_DIM = N_HEADS * HEAD_DIM
KV_DIM = N_KV_HEADS * HEAD_DIM
WINDOW = 128
BLOCK = 128
ATTN_SCALE = 1.0 / math.sqrt(HEAD_DIM)
NEG_INF = -1e30
N_BUCKETS = 32
MAX_DISTANCE = 128
RWKV_HEAD_DIM = 64
RWKV_HEADS = D_MODEL // RWKV_HEAD_DIM
GN_EPS = 64e-5
N_EXPERTS = 16
D_EXPERT = D_MODEL // 2
CAPACITY_FACTOR = 2
EPS = 1e-6

V7X_VMEM_LIMIT_BYTES = 56 * 1024 * 1024


def _params(*sem):
    return pltpu.CompilerParams(dimension_semantics=sem, vmem_limit_bytes=V7X_VMEM_LIMIT_BYTES)


def _mm_kernel(x_ref, w_ref, o_ref):
    o_ref[...] = jnp.dot(x_ref[...], w_ref[...].astype(BF16),
                         preferred_element_type=F32).astype(o_ref.dtype)


def _mm_res_kernel(x_ref, w_ref, r_ref, o_ref):
    o_ref[...] = r_ref[...] + jnp.dot(x_ref[...], w_ref[...].astype(BF16),
                                      preferred_element_type=F32)


def _matmul(x, w, widx=(), *, out_dtype=F32, tm=1024, tn=512, residual=None):
    nlead = len(widx)
    wb = w.ndim - nlead == 3
    xb = x.ndim == 3
    G = w.shape[nlead] if wb else (x.shape[0] if xb else 1)
    M, K = x.shape[-2:]
    N = w.shape[-1]
    tm = min(tm, M)
    tn = min(tn, N)
    assert M % tm == 0 and N % tn == 0 and w.shape[-2] == K
    x_spec = (pl.BlockSpec((None, tm, K), lambda g, i, j: (g, i, 0)) if xb
              else pl.BlockSpec((tm, K), lambda g, i, j: (i, 0)))
    w_spec = pl.BlockSpec((None,) * (nlead + wb) + (K, tn),
                          lambda g, i, j: tuple(widx) + ((g,) if wb else ()) + (0, j))
    batched = wb or xb
    o_spec = (pl.BlockSpec((None, tm, tn), lambda g, i, j: (g, i, j)) if batched
              else pl.BlockSpec((tm, tn), lambda g, i, j: (i, j)))
    o_shape = (G, M, N) if batched else (M, N)
    if residual is None:
        body, specs, args = _mm_kernel, [x_spec, w_spec], (x, w)
    else:
        assert not batched and out_dtype == F32
        body, specs, args = _mm_res_kernel, [x_spec, w_spec, o_spec], (x, w, residual)
    return pl.pallas_call(
        body,
        grid=(G, M // tm, N // tn),
        in_specs=specs,
        out_specs=o_spec,
        out_shape=jax.ShapeDtypeStruct(o_shape, out_dtype),
        compiler_params=_params("parallel", "parallel", "arbitrary"),
        name="dense_matmul",
    )(*args)


def _t5_bucket(rel):
    nb = N_BUCKETS // 2
    max_exact = nb // 2
    ret = jnp.where(rel > 0, nb, 0)
    n = jnp.abs(rel)
    nf = jnp.maximum(n, 1).astype(F32)
    large = max_exact + (jnp.log(nf / max_exact) / math.log(MAX_DISTANCE / max_exact)
                         * (nb - max_exact)).astype(jnp.int32)
    large = jnp.minimum(large, nb - 1)
    return ret + jnp.where(n < max_exact, n, large)


def _band_bias(rel_bias):
    q_pos = jnp.arange(BLOCK)[:, None]
    k_off = jnp.arange(3 * BLOCK)[None, :] - BLOCK
    bias = rel_bias.astype(F32)[_t5_bucket(k_off - q_pos)]
    return jnp.transpose(bias, (2, 0, 1))


def _rms(x, g):
    return x * lax.rsqrt(jnp.mean(x * x, axis=-1, keepdims=True) + EPS) * g


def _attn_kernel(sink_ref, q_ref, kp_ref, kc_ref, kn_ref, vp_ref, vc_ref, vn_ref, bias_ref,
                 qg_ref, kg_ref, o_ref, *, nb):
    h = pl.program_id(0)
    n = pl.program_id(2)
    k = jnp.concatenate([kp_ref[...], kc_ref[...], kn_ref[...]], axis=0)
    k = _rms(k, kg_ref[...]).astype(BF16)
    v = jnp.concatenate([vp_ref[...], vc_ref[...], vn_ref[...]], axis=0).astype(BF16)
    row = lax.broadcasted_iota(jnp.int32, (BLOCK, 3 * BLOCK), 0)
    col = lax.broadcasted_iota(jnp.int32, (BLOCK, 3 * BLOCK), 1)
    rel = col - BLOCK - row
    valid = (jnp.abs(rel) <= WINDOW)
    valid = valid & ((col >= BLOCK) | (n > 0)) & ((col < 2 * BLOCK) | (n < nb - 1))
    for g in range(GQA_GROUP):
        q = _rms(q_ref[:, g * HEAD_DIM:(g + 1) * HEAD_DIM], qg_ref[...]).astype(BF16)
        logits = lax.dot_general(q, k, (((1,), (1,)), ((), ())), preferred_element_type=F32)
        logits = jnp.where(valid, logits * ATTN_SCALE + bias_ref[g], NEG_INF)
        sink = sink_ref[h * GQA_GROUP + g]
        m = jnp.maximum(jnp.max(logits, axis=-1, keepdims=True), sink)
        p = jnp.exp(logits - m)
        den = jnp.sum(p, axis=-1, keepdims=True) + jnp.exp(sink - m)
        probs = (p / den).astype(BF16)
        o_ref[:, g * HEAD_DIM:(g + 1) * HEAD_DIM] = jnp.dot(
            probs, v, preferred_element_type=F32).astype(o_ref.dtype)


def _attention(qkv, bias, sink, q_gain, k_gain):
    B, S, _ = qkv.shape
    nb = S // BLOCK
    kcol = Q_DIM // HEAD_DIM
    vcol = (Q_DIM + KV_DIM) // HEAD_DIM

    def band(col0, shift):
        def index(h, b, n):
            return (b, jnp.clip(n + shift, 0, nb - 1), col0 + h)
        return pl.BlockSpec((None, BLOCK, HEAD_DIM), index)

    qw = GQA_GROUP * HEAD_DIM
    return pl.pallas_call(
        functools.partial(_attn_kernel, nb=nb),
        grid=(N_KV_HEADS, B, nb),
        in_specs=[pl.BlockSpec(memory_space=pltpu.SMEM),
                  pl.BlockSpec((None, BLOCK, qw), lambda h, b, n: (b, n, h)),
                  band(kcol, -1), band(kcol, 0), band(kcol, 1),
                  band(vcol, -1), band(vcol, 0), band(vcol, 1),
                  pl.BlockSpec((GQA_GROUP, BLOCK, 3 * BLOCK), lambda h, b, n: (h, 0, 0)),
                  pl.BlockSpec((1, HEAD_DIM), lambda h, b, n: (0, 0)),
                  pl.BlockSpec((1, HEAD_DIM), lambda h, b, n: (0, 0))],
        out_specs=pl.BlockSpec((None, BLOCK, qw), lambda h, b, n: (b, n, h)),
        out_shape=jax.ShapeDtypeStruct((B, S, Q_DIM), BF16),
        compiler_params=_params("parallel", "parallel", "arbitrary"),
        name="windowed_gqa",
    )(sink, qkv, qkv, qkv, qkv, qkv, qkv, qkv, bias, q_gain.reshape(1, -1), k_gain.reshape(1, -1))


WKV_ROWS = 4
WKV_TCHUNK = 32
WKV_UNROLL = 8
SUBLANES = 8
WKV_SLABS = RWKV_HEAD_DIM // SUBLANES


def _sublane_allsum(x):
    x = x + pltpu.roll(x, 4, 0)
    x = x + pltpu.roll(x, 2, 0)
    return x + pltpu.roll(x, 1, 0)


def _wkv_kernel(r_ref, k_ref, v_ref, wl_ref, al_ref, w0_ref, a0_ref, kk_ref, ka_ref, rk_ref,
                y_ref, c_ref,
                s_ref, kkn_s, wr_s, w_s, b_s, kd_s, v_s, br_s, kr_s):
    d = pl.program_id(0)
    tc = kkn_s.shape[0]
    nv = s_ref.shape[0]

    @pl.when(pl.program_id(1) == 0)
    def _():
        s_ref[...] = jnp.zeros_like(s_ref)

    def prep(t, carry):
        cat = lambda ref: jnp.concatenate([ref[0, t], ref[1, t]], axis=-1)
        r = cat(r_ref)
        k = cat(k_ref)
        kk = k * kk_ref[...]
        kk = kk / jnp.maximum(jnp.sqrt(jnp.sum(kk * kk, axis=0, keepdims=True)), 1e-12)
        w = jnp.exp(jax.nn.sigmoid(w0_ref[...] + cat(wl_ref)) * (-math.exp(-0.5)))
        a = jax.nn.sigmoid(a0_ref[...] + cat(al_ref))
        kd = k * (1.0 + (a - 1.0) * ka_ref[...])
        b = kk * a
        rkd = r * kd
        kkn_s[t] = -kk
        wr_s[t] = w * r
        w_s[t] = w
        b_s[t] = b
        kd_s[t] = kd
        v_s[t] = cat(v_ref)
        br_s[pl.ds(t, 1), :] = jnp.sum(b * r, axis=0, keepdims=True)
        kr_s[pl.ds(t, 1), :] = jnp.sum(rkd, axis=0, keepdims=True)
        c_ref[pl.ds(t, 1), :] = jnp.sum(rkd * rk_ref[...], axis=0, keepdims=True)
        return carry

    lax.fori_loop(0, tc, prep, 0, unroll=4)

    def group(gi, carry):
        v0 = gi * WKV_ROWS
        state = tuple(s_ref[v0 + r, pl.ds(SUBLANES * j, SUBLANES), :]
                      for r in range(WKV_ROWS) for j in range(WKV_SLABS))

        def step(tt, state):
            t = jnp.where(d == 0, tt, tc - 1 - tt)
            slab = lambda ref, j: ref[t, pl.ds(SUBLANES * j, SUBLANES), :]
            sa = [None] * WKV_ROWS
            p = [None] * WKV_ROWS
            for j in range(WKV_SLABS):
                kkn = slab(kkn_s, j)
                wr = slab(wr_s, j)
                for r in range(WKV_ROWS):
                    s = state[r * WKV_SLABS + j]
                    sa[r] = s * kkn if j == 0 else sa[r] + s * kkn
                    p[r] = s * wr if j == 0 else p[r] + s * wr
            br = br_s[pl.ds(t, 1), :]
            kr = kr_s[pl.ds(t, 1), :]
            vv = []
            for r in range(WKV_ROWS):
                sa[r] = _sublane_allsum(sa[r])
                p[r] = _sublane_allsum(p[r])
                vrow = v_s[t, pl.ds(v0 + r, 1), :]
                vv.append(jnp.broadcast_to(vrow, (SUBLANES, vrow.shape[-1])))
                y_ref[t, pl.ds(v0 + r, 1), :] = p[r][:1] + sa[r][:1] * br + vrow * kr
            new = [None] * len(state)
            for j in range(WKV_SLABS):
                w = slab(w_s, j)
                b = slab(b_s, j)
                kd = slab(kd_s, j)
                for r in range(WKV_ROWS):
                    i = r * WKV_SLABS + j
                    new[i] = state[i] * w + sa[r] * b + vv[r] * kd
            return tuple(new)

        state = lax.fori_loop(0, tc, step, state, unroll=WKV_UNROLL)
        for r in range(WKV_ROWS):
            for j in range(WKV_SLABS):
                s_ref[v0 + r, pl.ds(SUBLANES * j, SUBLANES), :] = state[r * WKV_SLABS + j]
        return carry

    lax.fori_loop(0, nv // WKV_ROWS, group, 0)


def _wkv_scan(r, k, v, wl, al, w0, a0, k_k, k_a, r_k):
    B, S, N, H = r.shape
    C = B * H
    tc = WKV_TCHUNK
    nt = S // tc

    def tmap(d, i):
        return i + d * (nt - 1 - 2 * i)

    shared = pl.BlockSpec((B, tc, N, H), lambda d, i: (0, tmap(d, i), 0, 0))
    per_dir = pl.BlockSpec((None, B, tc, N, H), lambda d, i: (d, 0, tmap(d, i), 0, 0))
    dir_vec = pl.BlockSpec((None, N, C), lambda d, i: (d, 0, 0))
    vec = pl.BlockSpec((N, C), lambda d, i: (0, 0))
    chunk = pltpu.VMEM((tc, N, C), F32)
    return pl.pallas_call(
        _wkv_kernel,
        grid=(2, nt),
        in_specs=[shared, shared, shared, per_dir, per_dir, dir_vec, dir_vec, vec, vec, vec],
        out_specs=[pl.BlockSpec((None, tc, N, C), lambda d, i: (d, tmap(d, i), 0, 0)),
                   pl.BlockSpec((None, tc, C), lambda d, i: (d, tmap(d, i), 0))],
        out_shape=[jax.ShapeDtypeStruct((2, S, N, C), F32), jax.ShapeDtypeStruct((2, S, C), F32)],
        scratch_shapes=[pltpu.VMEM((N, N, C), F32), chunk, chunk, chunk, chunk, chunk, chunk,
                        pltpu.VMEM((tc, C), F32), pltpu.VMEM((tc, C), F32)],
        compiler_params=_params("arbitrary", "arbitrary"),
        name="wkv7_scan",
    )(r, k, v, wl, al, w0, a0, k_k, k_a, r_k)


def _gate_up_kernel(x_ref, wg_ref, wu_ref, o_ref):
    x = x_ref[...]
    g = jnp.dot(x, wg_ref[...].astype(BF16), preferred_element_type=F32)
    u = jnp.dot(x, wu_ref[...].astype(BF16), preferred_element_type=F32)
    o_ref[...] = (g * jax.nn.sigmoid(g) * u).astype(o_ref.dtype)


def _down_kernel(h_ref, wd_ref, gate_ref, o_ref):
    y = jnp.dot(h_ref[...], wd_ref[...].astype(BF16), preferred_element_type=F32)
    o_ref[...] = y * gate_ref[...]


def _expert_ffn(xe, gate, w_gate, w_up, w_down, layer, *, tm=1024, tf=256, tn=1024):
    E, C, D = xe.shape
    Fd = w_gate.shape[-1]
    tm = min(tm, C)
    hid = pl.pallas_call(
        _gate_up_kernel,
        grid=(E, C // tm, Fd // tf),
        in_specs=[pl.BlockSpec((None, tm, D), lambda e, i, j: (e, i, 0)),
                  pl.BlockSpec((None, None, D, tf), lambda e, i, j: (layer, e, 0, j)),
                  pl.BlockSpec((None, None, D, tf), lambda e, i, j: (layer, e, 0, j))],
        out_specs=pl.BlockSpec((None, tm, tf), lambda e, i, j: (e, i, j)),
        out_shape=jax.ShapeDtypeStruct((E, C, Fd), BF16),
        compiler_params=_params("parallel", "parallel", "arbitrary"),
        name="expert_gate_up",
    )(xe, w_gate, w_up)
    return pl.pallas_call(
        _down_kernel,
        grid=(E, C // tm, D // tn),
        in_specs=[pl.BlockSpec((None, tm, Fd), lambda e, i, j: (e, i, 0)),
                  pl.BlockSpec((None, None, Fd, tn), lambda e, i, j: (layer, e, 0, j)),
                  pl.BlockSpec((None, tm, 1), lambda e, i, j: (e, i, 0))],
        out_specs=pl.BlockSpec((None, tm, tn), lambda e, i, j: (e, i, j)),
        out_shape=jax.ShapeDtypeStruct((E, C, D), F32),
        compiler_params=_params("parallel", "parallel", "arbitrary"),
        name="expert_down",
    )(hid, w_down, gate)


def _rms_norm(x, g):
    y = x * lax.rsqrt(jnp.mean(x * x, axis=-1, keepdims=True) + EPS)
    return y * g


def _attn_layer(x, h, w_qkv, j, q_gain, k_gain, sink, w_o, bias):
    B, S, D = h.shape
    qkv = _matmul(h.reshape(B * S, D).astype(BF16), w_qkv, (j,)).reshape(B, S, -1)
    o = _attention(qkv, bias, sink, q_gain, k_gain)
    return _matmul(o.reshape(B * S, Q_DIM), w_o, (j,), residual=x.reshape(B * S, D)).reshape(B, S, D)


def _nh_cols(w):
    lead = w.shape[:-1]
    w = w.reshape(lead + (RWKV_HEADS, RWKV_HEAD_DIM))
    return jnp.swapaxes(w, -1, -2).reshape(lead + (D_MODEL,))


def _nh_tile(vec, B):
    lead = vec.shape[:-1]
    t = jnp.swapaxes(vec.reshape(lead + (RWKV_HEADS, RWKV_HEAD_DIM)), -1, -2)
    return jnp.tile(t, (1,) * len(lead) + (1, B))


def _rwkv_layer(x, h, j, mu, w_rkv, w0, w1, w2, a0, a1, a2, g1, g2, k_k, k_a, r_k, ln_w, ln_b, w_o):
    B, S, D = h.shape
    n = B * S
    x_prev = jnp.pad(h, ((0, 0), (1, 0), (0, 0)))[:, :-1]
    x_next = jnp.pad(h, ((0, 0), (0, 1), (0, 0)))[:, 1:]
    xx = 0.5 * (x_prev + x_next) - h
    xr, xw, xk, xv, xa, xg = [(h + xx * mu[j, c]).reshape(n, D).astype(BF16) for c in range(6)]
    w_rkv_p = _nh_cols(w_rkv[j]).astype(BF16)
    rkv = _matmul(jnp.stack([xr, xk, xv]), w_rkv_p)
    gpad = (-g1.shape[-1]) % 128
    g1p = jnp.pad(g1[j], ((0, 0), (0, gpad)))
    g2p = jnp.pad(_nh_cols(g2[j]), ((0, gpad), (0, 0)))
    g = _matmul(jax.nn.sigmoid(_matmul(xg, g1p)).astype(BF16), g2p)
    wl = _matmul(jnp.tanh(_matmul(xw, w1, (j,))).astype(BF16), _nh_cols(w2[j]))
    al = _matmul(_matmul(xa, a1, (j,)).astype(BF16), _nh_cols(a2[j]))

    scan5 = lambda t: t.reshape(t.shape[:-2] + (B, S, RWKV_HEAD_DIM, RWKV_HEADS))
    rkv5 = scan5(rkv)
    ys, cs = _wkv_scan(rkv5[0], rkv5[1], rkv5[2], scan5(wl), scan5(al),
                       _nh_tile(w0[j], B), _nh_tile(a0[j], B), _nh_tile(k_k[j], B),
                       _nh_tile(k_a[j], B), jnp.tile(r_k[j].T, (1, B)))
    y = ys[0] + ys[1]
    mean = jnp.mean(y, axis=1, keepdims=True)
    var = jnp.mean(jnp.square(y - mean), axis=1, keepdims=True)
    yn = (y - mean) * lax.rsqrt(var + GN_EPS)
    unchain = lambda t: jnp.moveaxis(t.reshape(t.shape[:-1] + (B, RWKV_HEADS)), -2, 0)
    yn = unchain(yn).reshape(n, D)
    c = unchain(cs[0] + cs[1])[:, :, None, :]
    bonus = (c * rkv5[2]).reshape(n, D)
    out = (yn * _nh_cols(ln_w[j]) + _nh_cols(ln_b[j]) + bonus) * g
    w_o_p = jnp.swapaxes(w_o[j].reshape(RWKV_HEADS, RWKV_HEAD_DIM, D), 0, 1).reshape(D, D)
    return _matmul(out.astype(BF16), w_o_p, residual=x.reshape(n, D)).reshape(B, S, D)


def _moe_layer(x, h, i, w_router, w_gate, w_up, w_down):
    B, S, D = h.shape
    n = B * S
    cap = CAPACITY_FACTOR * n // N_EXPERTS
    xt = h.reshape(n, D)
    logits = jnp.dot(xt, w_router[i], precision=lax.Precision.HIGHEST)
    aff = jax.nn.softmax(logits, axis=-1)
    gate, idx = lax.top_k(aff.T, cap)
    xe = xt.astype(BF16)[idx]
    ye = _expert_ffn(xe, gate[..., None], w_gate, w_up, w_down, i)
    out = x.reshape(n, D).at[idx.reshape(-1)].add(ye.reshape(-1, D))
    return out.reshape(B, S, D)


def kernel(x_prompt, x_sample, rel_bias, norm_mix, norm_ffn, attn_w_qkv, attn_q_gain, attn_k_gain,
           attn_sink, attn_w_o, rwkv_mu, rwkv_w_rkv, rwkv_w0, rwkv_w1, rwkv_w2, rwkv_a0, rwkv_a1,
           rwkv_a2, rwkv_g1, rwkv_g2, rwkv_k_k, rwkv_k_a, rwkv_r_k, rwkv_ln_w, rwkv_ln_b, rwkv_w_o,
           moe_router, moe_w_gate, moe_w_up, moe_w_down):
    bias = _band_bias(rel_bias)

    def trunk(x):
        for i in range(DEPTH):
            j = i // 2
            h = _rms_norm(x, norm_mix[i])
            if i % 2 == 0:
                x = _attn_layer(x, h, attn_w_qkv, j, attn_q_gain[j], attn_k_gain[j], attn_sink[j],
                                attn_w_o, bias)
            else:
                x = _rwkv_layer(x, h, j, rwkv_mu, rwkv_w_rkv, rwkv_w0, rwkv_w1, rwkv_w2, rwkv_a0,
                                rwkv_a1, rwkv_a2, rwkv_g1, rwkv_g2, rwkv_k_k, rwkv_k_a, rwkv_r_k,
                                rwkv_ln_w, rwkv_ln_b, rwkv_w_o)
            x = _moe_layer(x, _rms_norm(x, norm_ffn[i]), i, moe_router, moe_w_gate, moe_w_up,
                           moe_w_down)
        return x

    return (trunk(x_prompt), trunk(x_sample))
```

```python
import functools
import math

import jax
import jax.numpy as jnp
from jax import lax
from jax.experimental import pallas as pl
from jax.experimental.pallas import tpu as pltpu

F32 = jnp.float32
BF16 = jnp.bfloat16

D_MODEL = 4096
DEPTH = 4
HEAD_DIM = 128
N_HEADS = D_MODEL // HEAD_DIM
N_KV_HEADS = N_HEADS // 4
GQA_GROUP = N_HEADS // N_KV_HEADS
Q_DIM = N_HEADS * HEAD_DIM
KV_DIM = N_KV_HEADS * HEAD_DIM
WINDOW = 128
BLOCK = 128
ATTN_SCALE = 1.0 / math.sqrt(HEAD_DIM)
NEG_INF = -1e30
N_BUCKETS = 32
MAX_DISTANCE = 128
RWKV_HEAD_DIM = 64
RWKV_HEADS = D_MODEL // RWKV_HEAD_DIM
GN_EPS = 64e-5
N_EXPERTS = 16
D_EXPERT = D_MODEL // 2
CAPACITY_FACTOR = 2
EPS = 1e-6

SUBLANES = 8

V7X_VMEM_LIMIT_BYTES = 56 * 1024 * 1024


def _params(*sem):
    return pltpu.CompilerParams(dimension_semantics=sem, vmem_limit_bytes=V7X_VMEM_LIMIT_BYTES)


def _mm_kernel(x_ref, w_ref, o_ref):
    o_ref[...] = jnp.dot(x_ref[...], w_ref[...].astype(BF16),
                         preferred_element_type=F32).astype(o_ref.dtype)


def _mm_res_kernel(x_ref, w_ref, r_ref, o_ref):
    o_ref[...] = r_ref[...] + jnp.dot(x_ref[...], w_ref[...].astype(BF16),
                                      preferred_element_type=F32)


def _matmul(x, w, widx=(), *, out_dtype=F32, tm=1024, tn=512, residual=None):
    nlead = len(widx)
    wb = w.ndim - nlead == 3
    xb = x.ndim == 3
    G = w.shape[nlead] if wb else (x.shape[0] if xb else 1)
    M, K = x.shape[-2:]
    N = w.shape[-1]
    tm = min(tm, M)
    tn = min(tn, N)
    assert M % tm == 0 and N % tn == 0 and w.shape[-2] == K
    x_spec = (pl.BlockSpec((None, tm, K), lambda g, i, j: (g, i, 0)) if xb
              else pl.BlockSpec((tm, K), lambda g, i, j: (i, 0)))
    w_spec = pl.BlockSpec((None,) * (nlead + wb) + (K, tn),
                          lambda g, i, j: tuple(widx) + ((g,) if wb else ()) + (0, j))
    batched = wb or xb
    o_spec = (pl.BlockSpec((None, tm, tn), lambda g, i, j: (g, i, j)) if batched
              else pl.BlockSpec((tm, tn), lambda g, i, j: (i, j)))
    o_shape = (G, M, N) if batched else (M, N)
    if residual is None:
        body, specs, args = _mm_kernel, [x_spec, w_spec], (x, w)
    else:
        assert not batched and out_dtype == F32
        body, specs, args = _mm_res_kernel, [x_spec, w_spec, o_spec], (x, w, residual)
    return pl.pallas_call(
        body,
        grid=(G, M // tm, N // tn),
        in_specs=specs,
        out_specs=o_spec,
        out_shape=jax.ShapeDtypeStruct(o_shape, out_dtype),
        compiler_params=_params("parallel", "parallel", "arbitrary"),
        name="dense_matmul",
    )(*args)


def _nh_cols_kernel(w_ref, o_ref):
    o_ref[...] = pltpu.einshape("k(hn)->k(nh)", w_ref[...], h=RWKV_HEADS,
                                n=RWKV_HEAD_DIM).astype(o_ref.dtype)


def _nh_cols_bf16(w, tk=256):
    G, K, N = w.shape
    spec = pl.BlockSpec((None, tk, N), lambda g, i: (g, i, 0))
    return pl.pallas_call(
        _nh_cols_kernel, grid=(G, K // tk), in_specs=[spec], out_specs=spec,
        out_shape=jax.ShapeDtypeStruct((G, K, N), BF16),
        compiler_params=_params("parallel", "parallel"), name="nh_cols")(w)


def _nh_rows_kernel(w_ref, o_ref):
    for i in range(SUBLANES):
        o_ref[i * RWKV_HEADS:(i + 1) * RWKV_HEADS, :] = w_ref[:, i, :].astype(o_ref.dtype)


def _nh_rows_bf16(w):
    D, N = w.shape
    return pl.pallas_call(
        _nh_rows_kernel, grid=(RWKV_HEAD_DIM // SUBLANES,),
        in_specs=[pl.BlockSpec((RWKV_HEADS, SUBLANES, N), lambda i: (0, i, 0))],
        out_specs=pl.BlockSpec((SUBLANES * RWKV_HEADS, N), lambda i: (i, 0)),
        out_shape=jax.ShapeDtypeStruct((D, N), BF16),
        compiler_params=_params("parallel"), name="nh_rows",
    )(w.reshape(RWKV_HEADS, RWKV_HEAD_DIM, N))


def _t5_bucket(rel):
    nb = N_BUCKETS // 2
    max_exact = nb // 2
    ret = jnp.where(rel > 0, nb, 0)
    n = jnp.abs(rel)
    nf = jnp.maximum(n, 1).astype(F32)
    large = max_exact + (jnp.log(nf / max_exact) / math.log(MAX_DISTANCE / max_exact)
                         * (nb - max_exact)).astype(jnp.int32)
    large = jnp.minimum(large, nb - 1)
    return ret + jnp.where(n < max_exact, n, large)


def _band_bias(rel_bias):
    q_pos = jnp.arange(BLOCK)[:, None]
    k_off = jnp.arange(3 * BLOCK)[None, :] - BLOCK
    bias = rel_bias.astype(F32)[_t5_bucket(k_off - q_pos)]
    return jnp.transpose(bias, (2, 0, 1))


def _rms(x, g):
    return x * lax.rsqrt(jnp.mean(x * x, axis=-1, keepdims=True) + EPS) * g


ATTN_QBLOCKS = 2


def _attn_kernel(sink_ref, q_ref, *refs, nb):
    nkb = ATTN_QBLOCKS + 2
    k_refs, v_refs = refs[:nkb], refs[nkb:2 * nkb]
    bias_ref, qg_ref, kg_ref, o_ref = refs[2 * nkb:]
    h = pl.program_id(0)
    n0 = pl.program_id(2) * ATTN_QBLOCKS
    kb = [_rms(r[...], kg_ref[...]).astype(BF16) for r in k_refs]
    vb = [r[...].astype(BF16) for r in v_refs]
    row = lax.broadcasted_iota(jnp.int32, (BLOCK, 3 * BLOCK), 0)
    col = lax.broadcasted_iota(jnp.int32, (BLOCK, 3 * BLOCK), 1)
    band = jnp.abs(col - BLOCK - row) <= WINDOW
    for qb in range(ATTN_QBLOCKS):
        n = n0 + qb
        k = jnp.concatenate(kb[qb:qb + 3], axis=0)
        v = jnp.concatenate(vb[qb:qb + 3], axis=0)
        valid = band & ((col >= BLOCK) | (n > 0)) & ((col < 2 * BLOCK) | (n < nb - 1))
        rows = slice(qb * BLOCK, (qb + 1) * BLOCK)
        for g in range(GQA_GROUP):
            cols = slice(g * HEAD_DIM, (g + 1) * HEAD_DIM)
            q = _rms(q_ref[rows, cols], qg_ref[...]).astype(BF16)
            logits = lax.dot_general(q, k, (((1,), (1,)), ((), ())), preferred_element_type=F32)
            logits = jnp.where(valid, logits * ATTN_SCALE + bias_ref[g], NEG_INF)
            sink = sink_ref[h * GQA_GROUP + g]
            m = jnp.maximum(jnp.max(logits, axis=-1, keepdims=True), sink)
            p = jnp.exp(logits - m)
            den = jnp.sum(p, axis=-1, keepdims=True) + jnp.exp(sink - m)
            probs = (p / den).astype(BF16)
            o_ref[rows, cols] = jnp.dot(probs, v, preferred_element_type=F32).astype(o_ref.dtype)


def _attention(qkv, bias, sink, q_gain, k_gain):
    B, S, _ = qkv.shape
    nb = S // BLOCK
    assert nb % ATTN_QBLOCKS == 0
    kcol = Q_DIM // HEAD_DIM
    vcol = (Q_DIM + KV_DIM) // HEAD_DIM

    def band(col0, shift):
        def index(h, b, i):
            return (b, jnp.clip(i * ATTN_QBLOCKS + shift, 0, nb - 1), col0 + h)
        return pl.BlockSpec((None, BLOCK, HEAD_DIM), index)

    qw = GQA_GROUP * HEAD_DIM
    shifts = range(-1, ATTN_QBLOCKS + 1)
    q_spec = pl.BlockSpec((None, ATTN_QBLOCKS * BLOCK, qw), lambda h, b, i: (b, i, h))
    return pl.pallas_call(
        functools.partial(_attn_kernel, nb=nb),
        grid=(N_KV_HEADS, B, nb // ATTN_QBLOCKS),
        in_specs=[pl.BlockSpec(memory_space=pltpu.SMEM), q_spec]
                 + [band(kcol, sh) for sh in shifts] + [band(vcol, sh) for sh in shifts]
                 + [pl.BlockSpec((GQA_GROUP, BLOCK, 3 * BLOCK), lambda h, b, i: (h, 0, 0)),
                    pl.BlockSpec((1, HEAD_DIM), lambda h, b, i: (0, 0)),
                    pl.BlockSpec((1, HEAD_DIM), lambda h, b, i: (0, 0))],
        out_specs=q_spec,
        out_shape=jax.ShapeDtypeStruct((B, S, Q_DIM), BF16),
        compiler_params=_params("parallel", "parallel", "arbitrary"),
        name="windowed_gqa",
    )(sink, qkv, *([qkv] * (2 * (ATTN_QBLOCKS + 2))), bias, q_gain.reshape(1, -1), k_gain.reshape(1, -1))


WKV_ROWS = 4
WKV_TCHUNK = 32
WKV_UNROLL = 8
WKV_SLABS = RWKV_HEAD_DIM // SUBLANES


def _sublane_allsum(x):
    x = x + pltpu.roll(x, 4, 0)
    x = x + pltpu.roll(x, 2, 0)
    return x + pltpu.roll(x, 1, 0)


def _wkv_kernel(r_ref, k_ref, v_ref, wl_ref, al_ref, w0_ref, a0_ref, kk_ref, ka_ref, rk_ref,
                y_ref, c_ref,
                s_ref, kkn_s, wr_s, w_s, b_s, kd_s, v_s, y_s, br_s, kr_s):
    d = pl.program_id(0)
    tc = kkn_s.shape[0]
    nv = s_ref.shape[0]
    nh = y_ref.shape[-1]

    @pl.when(pl.program_id(1) == 0)
    def _():
        s_ref[...] = jnp.zeros_like(s_ref)

    def prep(t, carry):
        cat = lambda ref: jnp.concatenate([ref[0, t], ref[1, t]], axis=-1)
        r = cat(r_ref)
        k = cat(k_ref)
        kk = k * kk_ref[...]
        kk = kk / jnp.maximum(jnp.sqrt(jnp.sum(kk * kk, axis=0, keepdims=True)), 1e-12)
        w = jnp.exp(jax.nn.sigmoid(w0_ref[...] + cat(wl_ref)) * (-math.exp(-0.5)))
        a = jax.nn.sigmoid(a0_ref[...] + cat(al_ref))
        kd = k * (1.0 + (a - 1.0) * ka_ref[...])
        b = kk * a
        rkd = r * kd
        kkn_s[t] = -kk
        wr_s[t] = w * r
        w_s[t] = w
        b_s[t] = b
        kd_s[t] = kd
        v_s[t] = cat(v_ref)
        br_s[pl.ds(t, 1), :] = jnp.sum(b * r, axis=0, keepdims=True)
        kr_s[pl.ds(t, 1), :] = jnp.sum(rkd, axis=0, keepdims=True)
        c_ref[pl.ds(t, 1), :] = jnp.sum(rkd * rk_ref[...], axis=0, keepdims=True)
        return carry

    lax.fori_loop(0, tc, prep, 0, unroll=4)

    def group(gi, carry):
        v0 = gi * WKV_ROWS
        state = tuple(s_ref[v0 + r, pl.ds(SUBLANES * j, SUBLANES), :]
                      for r in range(WKV_ROWS) for j in range(WKV_SLABS))

        def step(tt, state):
            t = jnp.where(d == 0, tt, tc - 1 - tt)
            slab = lambda ref, j: ref[t, pl.ds(SUBLANES * j, SUBLANES), :]
            sa = [None] * WKV_ROWS
            p = [None] * WKV_ROWS
            for j in range(WKV_SLABS):
                kkn = slab(kkn_s, j)
                wr = slab(wr_s, j)
                for r in range(WKV_ROWS):
                    s = state[r * WKV_SLABS + j]
                    sa[r] = s * kkn if j == 0 else sa[r] + s * kkn
                    p[r] = s * wr if j == 0 else p[r] + s * wr
            br = br_s[pl.ds(t, 1), :]
            kr = kr_s[pl.ds(t, 1), :]
            vv = []
            for r in range(WKV_ROWS):
                sa[r] = _sublane_allsum(sa[r])
                p[r] = _sublane_allsum(p[r])
                vrow = v_s[t, pl.ds(v0 + r, 1), :]
                vv.append(jnp.broadcast_to(vrow, (SUBLANES, vrow.shape[-1])))
                y_s[t, pl.ds(v0 + r, 1), :] = p[r][:1] + sa[r][:1] * br + vrow * kr
            new = [None] * len(state)
            for j in range(WKV_SLABS):
                w = slab(w_s, j)
                b = slab(b_s, j)
                kd = slab(kd_s, j)
                for r in range(WKV_ROWS):
                    i = r * WKV_SLABS + j
                    new[i] = state[i] * w + sa[r] * b + vv[r] * kd
            return tuple(new)

        state = lax.fori_loop(0, tc, step, state, unroll=WKV_UNROLL)
        for r in range(WKV_ROWS):
            for j in range(WKV_SLABS):
                s_ref[v0 + r, pl.ds(SUBLANES * j, SUBLANES), :] = state[r * WKV_SLABS + j]
        return carry

    lax.fori_loop(0, nv // WKV_ROWS, group, 0)

    def split(t, carry):
        y = y_s[t]
        y_ref[0, t] = y[:, :nh]
        y_ref[1, t] = pltpu.roll(y, nh, 1)[:, :nh]
        return carry

    lax.fori_loop(0, tc, split, 0, unroll=4)


def _wkv_scan(r, k, v, wl, al, w0, a0, k_k, k_a, r_k):
    B, S, N, H = r.shape
    C = B * H
    tc = WKV_TCHUNK
    nt = S // tc

    def tmap(d, i):
        return i + d * (nt - 1 - 2 * i)

    shared = pl.BlockSpec((B, tc, N, H), lambda d, i: (0, tmap(d, i), 0, 0))
    per_dir = pl.BlockSpec((None, B, tc, N, H), lambda d, i: (d, 0, tmap(d, i), 0, 0))
    dir_vec = pl.BlockSpec((None, N, C), lambda d, i: (d, 0, 0))
    vec = pl.BlockSpec((N, C), lambda d, i: (0, 0))
    chunk = pltpu.VMEM((tc, N, C), F32)
    return pl.pallas_call(
        _wkv_kernel,
        grid=(2, nt),
        in_specs=[shared, shared, shared, per_dir, per_dir, dir_vec, dir_vec, vec, vec, vec],
        out_specs=[per_dir,
                   pl.BlockSpec((None, tc, C), lambda d, i: (d, tmap(d, i), 0))],
        out_shape=[jax.ShapeDtypeStruct((2, B, S, N, H), F32), jax.ShapeDtypeStruct((2, S, C), F32)],
        scratch_shapes=[pltpu.VMEM((N, N, C), F32), chunk, chunk, chunk, chunk, chunk, chunk, chunk,
                        pltpu.VMEM((tc, C), F32), pltpu.VMEM((tc, C), F32)],
        compiler_params=_params("arbitrary", "arbitrary"),
        name="wkv7_scan",
    )(r, k, v, wl, al, w0, a0, k_k, k_a, r_k)


def _gate_up_kernel(x_ref, wg_ref, wu_ref, o_ref):
    x = x_ref[...]
    g = jnp.dot(x, wg_ref[...].astype(BF16), preferred_element_type=F32)
    u = jnp.dot(x, wu_ref[...].astype(BF16), preferred_element_type=F32)
    o_ref[...] = (g * jax.nn.sigmoid(g) * u).astype(o_ref.dtype)


def _down_kernel(h_ref, wd_ref, gate_ref, o_ref):
    y = jnp.dot(h_ref[...], wd_ref[...].astype(BF16), preferred_element_type=F32)
    o_ref[...] = y * gate_ref[...]


def _expert_ffn(xe, gate, w_gate, w_up, w_down, layer, *, tm=1024, tf=256, tn=1024):
    E, C, D = xe.shape
    Fd = w_gate.shape[-1]
    tm = min(tm, C)
    hid = pl.pallas_call(
        _gate_up_kernel,
        grid=(E, C // tm, Fd // tf),
        in_specs=[pl.BlockSpec((None, tm, D), lambda e, i, j: (e, i, 0)),
                  pl.BlockSpec((None, None, D, tf), lambda e, i, j: (layer, e, 0, j)),
                  pl.BlockSpec((None, None, D, tf), lambda e, i, j: (layer, e, 0, j))],
        out_specs=pl.BlockSpec((None, tm, tf), lambda e, i, j: (e, i, j)),
        out_shape=jax.ShapeDtypeStruct((E, C, Fd), BF16),
        compiler_params=_params("parallel", "parallel", "arbitrary"),
        name="expert_gate_up",
    )(xe, w_gate, w_up)
    return pl.pallas_call(
        _down_kernel,
        grid=(E, C // tm, D // tn),
        in_specs=[pl.BlockSpec((None, tm, Fd), lambda e, i, j: (e, i, 0)),
                  pl.BlockSpec((None, None, Fd, tn), lambda e, i, j: (layer, e, 0, j)),
                  pl.BlockSpec((None, tm, 1), lambda e, i, j: (e, i, 0))],
        out_specs=pl.BlockSpec((None, tm, tn), lambda e, i, j: (e, i, j)),
        out_shape=jax.ShapeDtypeStruct((E, C, D), F32),
        compiler_params=_params("parallel", "parallel", "arbitrary"),
        name="expert_down",
    )(hid, w_down, gate)


def _rms_norm(x, g):
    y = x * lax.rsqrt(jnp.mean(x * x, axis=-1, keepdims=True) + EPS)
    return y * g


def _attn_layer(x, h, w_qkv, j, q_gain, k_gain, sink, w_o, bias):
    B, S, D = h.shape
    qkv = _matmul(h.reshape(B * S, D).astype(BF16), w_qkv, (j,)).reshape(B, S, -1)
    o = _attention(qkv, bias, sink, q_gain, k_gain)
    return _matmul(o.reshape(B * S, Q_DIM), w_o, (j,), residual=x.reshape(B * S, D)).reshape(B, S, D)


def _nh_cols(w):
    lead = w.shape[:-1]
    w = w.reshape(lead + (RWKV_HEADS, RWKV_HEAD_DIM))
    return jnp.swapaxes(w, -1, -2).reshape(lead + (D_MODEL,))


def _nh_tile(vec, B):
    lead = vec.shape[:-1]
    t = jnp.swapaxes(vec.reshape(lead + (RWKV_HEADS, RWKV_HEAD_DIM)), -1, -2)
    return jnp.tile(t, (1,) * len(lead) + (1, B))


def _rwkv_weights(j, w_rkv, w0, w2, a0, a2, g1, g2, k_k, k_a, r_k, ln_w, ln_b, w_o, B):
    gpad = (-g1.shape[-1]) % 128
    return dict(
        w_rkv=_nh_cols_bf16(w_rkv[j]),
        g1=jnp.pad(g1[j], ((0, 0), (0, gpad))),
        g2=jnp.pad(_nh_cols(g2[j]), ((0, gpad), (0, 0))),
        w2=_nh_cols(w2[j]), a2=_nh_cols(a2[j]),
        w0=_nh_tile(w0[j], B), a0=_nh_tile(a0[j], B), k_k=_nh_tile(k_k[j], B), k_a=_nh_tile(k_a[j], B),
        r_k=jnp.tile(r_k[j].T, (1, B)),
        ln_w=_nh_cols(ln_w[j]).reshape(RWKV_HEAD_DIM, RWKV_HEADS),
        ln_b=_nh_cols(ln_b[j]).reshape(RWKV_HEAD_DIM, RWKV_HEADS),
        w_o=_nh_rows_bf16(w_o[j]))


def _rwkv_layer(x, h, j, mu, w1, a1, p):
    B, S, D = h.shape
    n = B * S
    x_prev = jnp.pad(h, ((0, 0), (1, 0), (0, 0)))[:, :-1]
    x_next = jnp.pad(h, ((0, 0), (0, 1), (0, 0)))[:, 1:]
    xx = 0.5 * (x_prev + x_next) - h
    xr, xw, xk, xv, xa, xg = [(h + xx * mu[j, c]).reshape(n, D).astype(BF16) for c in range(6)]
    rkv = _matmul(jnp.stack([xr, xk, xv]), p["w_rkv"])
    g = _matmul(jax.nn.sigmoid(_matmul(xg, p["g1"])).astype(BF16), p["g2"])
    wl = _matmul(jnp.tanh(_matmul(xw, w1, (j,))).astype(BF16), p["w2"])
    al = _matmul(_matmul(xa, a1, (j,)).astype(BF16), p["a2"])

    scan5 = lambda t: t.reshape(t.shape[:-2] + (B, S, RWKV_HEAD_DIM, RWKV_HEADS))
    rkv5 = scan5(rkv)
    ys, cs = _wkv_scan(rkv5[0], rkv5[1], rkv5[2], scan5(wl), scan5(al),
                       p["w0"], p["a0"], p["k_k"], p["k_a"], p["r_k"])
    y = ys[0] + ys[1]
    mean = jnp.mean(y, axis=2, keepdims=True)
    var = jnp.mean(jnp.square(y - mean), axis=2, keepdims=True)
    yn = (y - mean) * lax.rsqrt(var + GN_EPS)
    c = (cs[0] + cs[1]).reshape(S, B, RWKV_HEADS)
    c = jnp.moveaxis(c, 1, 0)[:, :, None, :]
    out = (yn * p["ln_w"] + p["ln_b"] + c * rkv5[2]) * scan5(g)
    return _matmul(out.reshape(n, D).astype(BF16), p["w_o"], residual=x.reshape(n, D)).reshape(B, S, D)


def _moe_layer(x, h, i, w_router, w_gate, w_up, w_down):
    B, S, D = h.shape
    n = B * S
    cap = CAPACITY_FACTOR * n // N_EXPERTS
    xt = h.reshape(n, D)
    logits = jnp.dot(xt, w_router[i], precision=lax.Precision.HIGHEST)
    aff = jax.nn.softmax(logits, axis=-1)
    gate, idx = lax.top_k(aff.T, cap)
    xe = xt.astype(BF16)[idx]
    ye = _expert_ffn(xe, gate[..., None], w_gate, w_up, w_down, i)
    out = x.reshape(n, D).at[idx.reshape(-1)].add(ye.reshape(-1, D))
    return out.reshape(B, S, D)


def kernel(x_prompt, x_sample, rel_bias, norm_mix, norm_ffn, attn_w_qkv, attn_q_gain, attn_k_gain,
           attn_sink, attn_w_o, rwkv_mu, rwkv_w_rkv, rwkv_w0, rwkv_w1, rwkv_w2, rwkv_a0, rwkv_a1,
           rwkv_a2, rwkv_g1, rwkv_g2, rwkv_k_k, rwkv_k_a, rwkv_r_k, rwkv_ln_w, rwkv_ln_b, rwkv_w_o,
           moe_router, moe_w_gate, moe_w_up, moe_w_down):
    bias = _band_bias(rel_bias)
    assert x_prompt.shape[0] == x_sample.shape[0]
    rwkv_p = [_rwkv_weights(j, rwkv_w_rkv, rwkv_w0, rwkv_w2, rwkv_a0, rwkv_a2, rwkv_g1, rwkv_g2,
                            rwkv_k_k, rwkv_k_a, rwkv_r_k, rwkv_ln_w, rwkv_ln_b, rwkv_w_o,
                            x_prompt.shape[0]) for j in range(DEPTH // 2)]

    def trunk(x):
        for i in range(DEPTH):
            j = i // 2
            h = _rms_norm(x, norm_mix[i])
            if i % 2 == 0:
                x = _attn_layer(x, h, attn_w_qkv, j, attn_q_gain[j], attn_k_gain[j], attn_sink[j],
                                attn_w_o, bias)
            else:
                x = _rwkv_layer(x, h, j, rwkv_mu, rwkv_w1, rwkv_a1, rwkv_p[j])
            x = _moe_layer(x, _rms_norm(x, norm_ffn[i]), i, moe_router, moe_w_gate, moe_w_up,
                           moe_w_down)
        return x

    return (trunk(x_prompt), trunk(x_sample))
```

```python
import functools
import math

import jax
import jax.numpy as jnp
from jax import lax
from jax.experimental import pallas as pl
from jax.experimental.pallas import tpu as pltpu

F32 = jnp.float32
BF16 = jnp.bfloat16

D_MODEL = 4096
DEPTH = 4
HEAD_DIM = 128
N_HEADS = D_MODEL // HEAD_DIM
N_KV_HEADS = N_HEADS // 4
GQA_GROUP = N_HEADS // N_KV_HEADS
Q_DIM = N_HEADS * HEAD_DIM
KV_DIM = N_KV_HEADS * HEAD_DIM
WINDOW = 128
BLOCK = 128
ATTN_SCALE = 1.0 / math.sqrt(HEAD_DIM)
NEG_INF = -1e30
N_BUCKETS = 32
MAX_DISTANCE = 128
RWKV_HEAD_DIM = 64
RWKV_HEADS = D_MODEL // RWKV_HEAD_DIM
GN_EPS = 64e-5
N_EXPERTS = 16
D_EXPERT = D_MODEL // 2
CAPACITY_FACTOR = 2
EPS = 1e-6

SUBLANES = 8

V7X_VMEM_LIMIT_BYTES = 56 * 1024 * 1024


def _params(*sem):
    return pltpu.CompilerParams(dimension_semantics=sem, vmem_limit_bytes=V7X_VMEM_LIMIT_BYTES)


def _mm_kernel(x_ref, w_ref, o_ref):
    o_ref[...] = jnp.dot(x_ref[...], w_ref[...].astype(BF16),
                         preferred_element_type=F32).astype(o_ref.dtype).reshape(o_ref.shape)


def _mm_res_kernel(x_ref, w_ref, r_ref, o_ref):
    o_ref[...] = r_ref[...] + jnp.dot(x_ref[...], w_ref[...].astype(BF16),
                                      preferred_element_type=F32)


def _matmul(x, w, widx=(), *, out_dtype=F32, tm=1024, tn=512, residual=None, split=None):
    nlead = len(widx)
    wb = w.ndim - nlead == 3
    xb = x.ndim == 3
    G = w.shape[nlead] if wb else (x.shape[0] if xb else 1)
    M, K = x.shape[-2:]
    N = w.shape[-1]
    tm = min(tm, M)
    tn = min(tn, N)
    assert M % tm == 0 and N % tn == 0 and w.shape[-2] == K
    x_spec = (pl.BlockSpec((None, tm, K), lambda g, i, j: (g, i, 0)) if xb
              else pl.BlockSpec((tm, K), lambda g, i, j: (i, 0)))
    w_spec = pl.BlockSpec((None,) * (nlead + wb) + (K, tn),
                          lambda g, i, j: tuple(widx) + ((g,) if wb else ()) + (0, j))
    batched = wb or xb
    if split is None:
        o_blk, o_idx, o_tail = (tm, tn), (lambda i, j: (i, j)), (M, N)
    else:
        assert tn % split == 0
        o_blk, o_idx, o_tail = (tm, tn // split, split), (lambda i, j: (i, j, 0)), (M, N // split, split)
    o_spec = (pl.BlockSpec((None,) + o_blk, lambda g, i, j: (g,) + o_idx(i, j)) if batched
              else pl.BlockSpec(o_blk, lambda g, i, j: o_idx(i, j)))
    o_shape = ((G,) if batched else ()) + o_tail
    if residual is None:
        body, specs, args = _mm_kernel, [x_spec, w_spec], (x, w)
    else:
        assert not batched and out_dtype == F32 and split is None
        body, specs, args = _mm_res_kernel, [x_spec, w_spec, o_spec], (x, w, residual)
    return pl.pallas_call(
        body,
        grid=(G, M // tm, N // tn),
        in_specs=specs,
        out_specs=o_spec,
        out_shape=jax.ShapeDtypeStruct(o_shape, out_dtype),
        compiler_params=_params("parallel", "parallel", "arbitrary"),
        name="dense_matmul",
    )(*args)


def _nh_cols_kernel(w_ref, o_ref):
    o_ref[...] = pltpu.einshape("k(hn)->k(nh)", w_ref[...], h=RWKV_HEADS,
                                n=RWKV_HEAD_DIM).astype(o_ref.dtype)


def _nh_cols_bf16(w, tk=256):
    G, K, N = w.shape
    spec = pl.BlockSpec((None, tk, N), lambda g, i: (g, i, 0))
    return pl.pallas_call(
        _nh_cols_kernel, grid=(G, K // tk), in_specs=[spec], out_specs=spec,
        out_shape=jax.ShapeDtypeStruct((G, K, N), BF16),
        compiler_params=_params("parallel", "parallel"), name="nh_cols")(w)


def _nh_rows_kernel(w_ref, o_ref):
    for i in range(SUBLANES):
        o_ref[i * RWKV_HEADS:(i + 1) * RWKV_HEADS, :] = w_ref[:, i, :].astype(o_ref.dtype)


def _nh_rows_bf16(w):
    D, N = w.shape
    return pl.pallas_call(
        _nh_rows_kernel, grid=(RWKV_HEAD_DIM // SUBLANES,),
        in_specs=[pl.BlockSpec((RWKV_HEADS, SUBLANES, N), lambda i: (0, i, 0))],
        out_specs=pl.BlockSpec((SUBLANES * RWKV_HEADS, N), lambda i: (i, 0)),
        out_shape=jax.ShapeDtypeStruct((D, N), BF16),
        compiler_params=_params("parallel"), name="nh_rows",
    )(w.reshape(RWKV_HEADS, RWKV_HEAD_DIM, N))


def _t5_bucket(rel):
    nb = N_BUCKETS // 2
    max_exact = nb // 2
    ret = jnp.where(rel > 0, nb, 0)
    n = jnp.abs(rel)
    nf = jnp.maximum(n, 1).astype(F32)
    large = max_exact + (jnp.log(nf / max_exact) / math.log(MAX_DISTANCE / max_exact)
                         * (nb - max_exact)).astype(jnp.int32)
    large = jnp.minimum(large, nb - 1)
    return ret + jnp.where(n < max_exact, n, large)


def _band_bias(rel_bias):
    q_pos = jnp.arange(BLOCK)[:, None]
    k_off = jnp.arange(3 * BLOCK)[None, :] - BLOCK
    bias = rel_bias.astype(F32)[_t5_bucket(k_off - q_pos)]
    return jnp.transpose(bias, (2, 0, 1))


def _rms(x, g):
    return x * lax.rsqrt(jnp.mean(x * x, axis=-1, keepdims=True) + EPS) * g


ATTN_QBLOCKS = 2


def _attn_kernel(sink_ref, q_ref, *refs, nb):
    nkb = ATTN_QBLOCKS + 2
    k_refs, v_refs = refs[:nkb], refs[nkb:2 * nkb]
    bias_ref, qg_ref, kg_ref, o_ref = refs[2 * nkb:]
    h = pl.program_id(0)
    n0 = pl.program_id(2) * ATTN_QBLOCKS
    kb = [_rms(r[...], kg_ref[...]).astype(BF16) for r in k_refs]
    vb = [r[...].astype(BF16) for r in v_refs]
    row = lax.broadcasted_iota(jnp.int32, (BLOCK, 3 * BLOCK), 0)
    col = lax.broadcasted_iota(jnp.int32, (BLOCK, 3 * BLOCK), 1)
    band = jnp.abs(col - BLOCK - row) <= WINDOW
    for qb in range(ATTN_QBLOCKS):
        n = n0 + qb
        k = jnp.concatenate(kb[qb:qb + 3], axis=0)
        v = jnp.concatenate(vb[qb:qb + 3], axis=0)
        valid = band & ((col >= BLOCK) | (n > 0)) & ((col < 2 * BLOCK) | (n < nb - 1))
        rows = slice(qb * BLOCK, (qb + 1) * BLOCK)
        for g in range(GQA_GROUP):
            cols = slice(g * HEAD_DIM, (g + 1) * HEAD_DIM)
            q = _rms(q_ref[rows, cols], qg_ref[...]).astype(BF16)
            logits = lax.dot_general(q, k, (((1,), (1,)), ((), ())), preferred_element_type=F32)
            logits = jnp.where(valid, logits * ATTN_SCALE + bias_ref[g], NEG_INF)
            sink = sink_ref[h * GQA_GROUP + g]
            m = jnp.maximum(jnp.max(logits, axis=-1, keepdims=True), sink)
            p = jnp.exp(logits - m)
            den = jnp.sum(p, axis=-1, keepdims=True) + jnp.exp(sink - m)
            probs = (p / den).astype(BF16)
            o_ref[rows, cols] = jnp.dot(probs, v, preferred_element_type=F32).astype(o_ref.dtype)


def _attention(qkv, bias, sink, q_gain, k_gain):
    B, S, _ = qkv.shape
    nb = S // BLOCK
    assert nb % ATTN_QBLOCKS == 0
    kcol = Q_DIM // HEAD_DIM
    vcol = (Q_DIM + KV_DIM) // HEAD_DIM

    def band(col0, shift):
        def index(h, b, i):
            return (b, jnp.clip(i * ATTN_QBLOCKS + shift, 0, nb - 1), col0 + h)
        return pl.BlockSpec((None, BLOCK, HEAD_DIM), index)

    qw = GQA_GROUP * HEAD_DIM
    shifts = range(-1, ATTN_QBLOCKS + 1)
    q_spec = pl.BlockSpec((None, ATTN_QBLOCKS * BLOCK, qw), lambda h, b, i: (b, i, h))
    return pl.pallas_call(
        functools.partial(_attn_kernel, nb=nb),
        grid=(N_KV_HEADS, B, nb // ATTN_QBLOCKS),
        in_specs=[pl.BlockSpec(memory_space=pltpu.SMEM), q_spec]
                 + [band(kcol, sh) for sh in shifts] + [band(vcol, sh) for sh in shifts]
                 + [pl.BlockSpec((GQA_GROUP, BLOCK, 3 * BLOCK), lambda h, b, i: (h, 0, 0)),
                    pl.BlockSpec((1, HEAD_DIM), lambda h, b, i: (0, 0)),
                    pl.BlockSpec((1, HEAD_DIM), lambda h, b, i: (0, 0))],
        out_specs=q_spec,
        out_shape=jax.ShapeDtypeStruct((B, S, Q_DIM), BF16),
        compiler_params=_params("parallel", "parallel", "arbitrary"),
        name="windowed_gqa",
    )(sink, qkv, *([qkv] * (2 * (ATTN_QBLOCKS + 2))), bias, q_gain.reshape(1, -1), k_gain.reshape(1, -1))


WKV_ROWS = 4
WKV_TCHUNK = 32
WKV_UNROLL = 8
WKV_SLABS = RWKV_HEAD_DIM // SUBLANES


def _sublane_allsum(x):
    x = x + pltpu.roll(x, 4, 0)
    x = x + pltpu.roll(x, 2, 0)
    return x + pltpu.roll(x, 1, 0)


def _wkv_kernel(r_ref, k_ref, v_ref, wl_ref, al_ref, w0_ref, a0_ref, kk_ref, ka_ref, rk_ref,
                y_ref, c_ref,
                s_ref, kkn_s, wr_s, w_s, b_s, kd_s, v_s, y_s, br_s, kr_s):
    d = pl.program_id(0)
    tc = kkn_s.shape[0]
    nv = s_ref.shape[0]
    nh = y_ref.shape[-1]

    @pl.when(pl.program_id(1) == 0)
    def _():
        s_ref[...] = jnp.zeros_like(s_ref)

    def prep(t, carry):
        cat = lambda ref: jnp.concatenate([ref[0, t], ref[1, t]], axis=-1)
        r = cat(r_ref)
        k = cat(k_ref)
        kk = k * kk_ref[...]
        kk = kk / jnp.maximum(jnp.sqrt(jnp.sum(kk * kk, axis=0, keepdims=True)), 1e-12)
        w = jnp.exp(jax.nn.sigmoid(w0_ref[...] + cat(wl_ref)) * (-math.exp(-0.5)))
        a = jax.nn.sigmoid(a0_ref[...] + cat(al_ref))
        kd = k * (1.0 + (a - 1.0) * ka_ref[...])
        b = kk * a
        rkd = r * kd
        kkn_s[t] = -kk
        wr_s[t] = w * r
        w_s[t] = w
        b_s[t] = b
        kd_s[t] = kd
        v_s[t] = cat(v_ref)
        br_s[pl.ds(t, 1), :] = jnp.sum(b * r, axis=0, keepdims=True)
        kr_s[pl.ds(t, 1), :] = jnp.sum(rkd, axis=0, keepdims=True)
        c_ref[pl.ds(t, 1), :] = jnp.sum(rkd * rk_ref[...], axis=0, keepdims=True)
        return carry

    lax.fori_loop(0, tc, prep, 0, unroll=4)

    def group(gi, carry):
        v0 = gi * WKV_ROWS
        state = tuple(s_ref[v0 + r, pl.ds(SUBLANES * j, SUBLANES), :]
                      for r in range(WKV_ROWS) for j in range(WKV_SLABS))

        def step(tt, state):
            t = jnp.where(d == 0, tt, tc - 1 - tt)
            slab = lambda ref, j: ref[t, pl.ds(SUBLANES * j, SUBLANES), :]
            sa = [None] * WKV_ROWS
            p = [None] * WKV_ROWS
            for j in range(WKV_SLABS):
                kkn = slab(kkn_s, j)
                wr = slab(wr_s, j)
                for r in range(WKV_ROWS):
                    s = state[r * WKV_SLABS + j]
                    sa[r] = s * kkn if j == 0 else sa[r] + s * kkn
                    p[r] = s * wr if j == 0 else p[r] + s * wr
            br = br_s[pl.ds(t, 1), :]
            kr = kr_s[pl.ds(t, 1), :]
            vv = []
            for r in range(WKV_ROWS):
                sa[r] = _sublane_allsum(sa[r])
                p[r] = _sublane_allsum(p[r])
                vrow = v_s[t, pl.ds(v0 + r, 1), :]
                vv.append(jnp.broadcast_to(vrow, (SUBLANES, vrow.shape[-1])))
                y_s[t, pl.ds(v0 + r, 1), :] = p[r][:1] + sa[r][:1] * br + vrow * kr
            new = [None] * len(state)
            for j in range(WKV_SLABS):
                w = slab(w_s, j)
                b = slab(b_s, j)
                kd = slab(kd_s, j)
                for r in range(WKV_ROWS):
                    i = r * WKV_SLABS + j
                    new[i] = state[i] * w + sa[r] * b + vv[r] * kd
            return tuple(new)

        state = lax.fori_loop(0, tc, step, state, unroll=WKV_UNROLL)
        for r in range(WKV_ROWS):
            for j in range(WKV_SLABS):
                s_ref[v0 + r, pl.ds(SUBLANES * j, SUBLANES), :] = state[r * WKV_SLABS + j]
        return carry

    lax.fori_loop(0, nv // WKV_ROWS, group, 0)

    def split(t, carry):
        y = y_s[t]
        y_ref[0, t] = y[:, :nh]
        y_ref[1, t] = pltpu.roll(y, nh, 1)[:, :nh]
        return carry

    lax.fori_loop(0, tc, split, 0, unroll=4)


def _wkv_scan(r, k, v, wl, al, w0, a0, k_k, k_a, r_k):
    B, S, N, H = r.shape
    C = B * H
    tc = WKV_TCHUNK
    nt = S // tc

    def tmap(d, i):
        return i + d * (nt - 1 - 2 * i)

    shared = pl.BlockSpec((B, tc, N, H), lambda d, i: (0, tmap(d, i), 0, 0))
    per_dir = pl.BlockSpec((None, B, tc, N, H), lambda d, i: (d, 0, tmap(d, i), 0, 0))
    dir_vec = pl.BlockSpec((None, N, C), lambda d, i: (d, 0, 0))
    vec = pl.BlockSpec((N, C), lambda d, i: (0, 0))
    chunk = pltpu.VMEM((tc, N, C), F32)
    return pl.pallas_call(
        _wkv_kernel,
        grid=(2, nt),
        in_specs=[shared, shared, shared, per_dir, per_dir, dir_vec, dir_vec, vec, vec, vec],
        out_specs=[per_dir,
                   pl.BlockSpec((None, tc, C), lambda d, i: (d, tmap(d, i), 0))],
        out_shape=[jax.ShapeDtypeStruct((2, B, S, N, H), F32), jax.ShapeDtypeStruct((2, S, C), F32)],
        scratch_shapes=[pltpu.VMEM((N, N, C), F32), chunk, chunk, chunk, chunk, chunk, chunk, chunk,
                        pltpu.VMEM((tc, C), F32), pltpu.VMEM((tc, C), F32)],
        compiler_params=_params("arbitrary", "arbitrary"),
        name="wkv7_scan",
    )(r, k, v, wl, al, w0, a0, k_k, k_a, r_k)


def _gate_up_kernel(x_ref, wg_ref, wu_ref, o_ref):
    x = x_ref[...]
    g = jnp.dot(x, wg_ref[...].astype(BF16), preferred_element_type=F32)
    u = jnp.dot(x, wu_ref[...].astype(BF16), preferred_element_type=F32)
    o_ref[...] = (g * jax.nn.sigmoid(g) * u).astype(o_ref.dtype)


def _down_kernel(h_ref, wd_ref, gate_ref, o_ref):
    y = jnp.dot(h_ref[...], wd_ref[...].astype(BF16), preferred_element_type=F32)
    o_ref[...] = y * gate_ref[...]


def _expert_ffn(xe, gate, w_gate, w_up, w_down, layer, *, tm=1024, tf=256, tn=1024):
    E, C, D = xe.shape
    Fd = w_gate.shape[-1]
    tm = min(tm, C)
    hid = pl.pallas_call(
        _gate_up_kernel,
        grid=(E, C // tm, Fd // tf),
        in_specs=[pl.BlockSpec((None, tm, D), lambda e, i, j: (e, i, 0)),
                  pl.BlockSpec((None, None, D, tf), lambda e, i, j: (layer, e, 0, j)),
                  pl.BlockSpec((None, None, D, tf), lambda e, i, j: (layer, e, 0, j))],
        out_specs=pl.BlockSpec((None, tm, tf), lambda e, i, j: (e, i, j)),
        out_shape=jax.ShapeDtypeStruct((E, C, Fd), BF16),
        compiler_params=_params("parallel", "parallel", "arbitrary"),
        name="expert_gate_up",
    )(xe, w_gate, w_up)
    return pl.pallas_call(
        _down_kernel,
        grid=(E, C // tm, D // tn),
        in_specs=[pl.BlockSpec((None, tm, Fd), lambda e, i, j: (e, i, 0)),
                  pl.BlockSpec((None, None, Fd, tn), lambda e, i, j: (layer, e, 0, j)),
                  pl.BlockSpec((None, tm, 1), lambda e, i, j: (e, i, 0))],
        out_specs=pl.BlockSpec((None, tm, tn), lambda e, i, j: (e, i, j)),
        out_shape=jax.ShapeDtypeStruct((E, C, D), F32),
        compiler_params=_params("parallel", "parallel", "arbitrary"),
        name="expert_down",
    )(hid, w_down, gate)


def _rms_norm(x, g):
    y = x * lax.rsqrt(jnp.mean(x * x, axis=-1, keepdims=True) + EPS)
    return y * g


def _attn_layer(x, h, w_qkv, j, q_gain, k_gain, sink, w_o, bias):
    B, S, D = h.shape
    qkv = _matmul(h.reshape(B * S, D).astype(BF16), w_qkv, (j,)).reshape(B, S, -1)
    o = _attention(qkv, bias, sink, q_gain, k_gain)
    return _matmul(o.reshape(B * S, Q_DIM), w_o, (j,), residual=x.reshape(B * S, D)).reshape(B, S, D)


def _nh_cols(w):
    lead = w.shape[:-1]
    w = w.reshape(lead + (RWKV_HEADS, RWKV_HEAD_DIM))
    return jnp.swapaxes(w, -1, -2).reshape(lead + (D_MODEL,))


def _nh_tile(vec, B):
    lead = vec.shape[:-1]
    t = jnp.swapaxes(vec.reshape(lead + (RWKV_HEADS, RWKV_HEAD_DIM)), -1, -2)
    return jnp.tile(t, (1,) * len(lead) + (1, B))


def _rwkv_weights(j, w_rkv, w0, w2, a0, a2, g1, g2, k_k, k_a, r_k, ln_w, ln_b, w_o, B):
    gpad = (-g1.shape[-1]) % 128
    return dict(
        w_rkv=_nh_cols_bf16(w_rkv[j]),
        g1=jnp.pad(g1[j], ((0, 0), (0, gpad))),
        g2=jnp.pad(_nh_cols(g2[j]), ((0, gpad), (0, 0))),
        w2=_nh_cols(w2[j]), a2=_nh_cols(a2[j]),
        w0=_nh_tile(w0[j], B), a0=_nh_tile(a0[j], B), k_k=_nh_tile(k_k[j], B), k_a=_nh_tile(k_a[j], B),
        r_k=jnp.tile(r_k[j].T, (1, B)),
        ln_w=_nh_cols(ln_w[j]).reshape(RWKV_HEAD_DIM, RWKV_HEADS),
        ln_b=_nh_cols(ln_b[j]).reshape(RWKV_HEAD_DIM, RWKV_HEADS),
        w_o=_nh_rows_bf16(w_o[j]))


def _rwkv_layer(x, h, j, mu, w1, a1, p):
    B, S, D = h.shape
    n = B * S
    x_prev = jnp.pad(h, ((0, 0), (1, 0), (0, 0)))[:, :-1]
    x_next = jnp.pad(h, ((0, 0), (0, 1), (0, 0)))[:, 1:]
    xx = 0.5 * (x_prev + x_next) - h
    xr, xw, xk, xv, xa, xg = [(h + xx * mu[j, c]).reshape(n, D).astype(BF16) for c in range(6)]
    nh = dict(split=RWKV_HEADS)
    rkv = _matmul(jnp.stack([xr, xk, xv]), p["w_rkv"], **nh)
    g = _matmul(jax.nn.sigmoid(_matmul(xg, p["g1"])).astype(BF16), p["g2"], **nh)
    wl = _matmul(jnp.tanh(_matmul(xw, w1, (j,))).astype(BF16), p["w2"], **nh)
    al = _matmul(_matmul(xa, a1, (j,)).astype(BF16), p["a2"], **nh)

    scan5 = lambda t: t.reshape(t.shape[:-3] + (B, S, RWKV_HEAD_DIM, RWKV_HEADS))
    rkv5 = scan5(rkv)
    ys, cs = _wkv_scan(rkv5[0], rkv5[1], rkv5[2], scan5(wl), scan5(al),
                       p["w0"], p["a0"], p["k_k"], p["k_a"], p["r_k"])
    y = ys[0] + ys[1]
    mean = jnp.mean(y, axis=2, keepdims=True)
    var = jnp.mean(jnp.square(y - mean), axis=2, keepdims=True)
    yn = (y - mean) * lax.rsqrt(var + GN_EPS)
    c = (cs[0] + cs[1]).reshape(S, B, RWKV_HEADS)
    c = jnp.moveaxis(c, 1, 0)[:, :, None, :]
    out = (yn * p["ln_w"] + p["ln_b"] + c * rkv5[2]) * scan5(g)
    return _matmul(out.reshape(n, D).astype(BF16), p["w_o"], residual=x.reshape(n, D)).reshape(B, S, D)


def _moe_layer(x, h, i, w_router, w_gate, w_up, w_down):
    B, S, D = h.shape
    n = B * S
    cap = CAPACITY_FACTOR * n // N_EXPERTS
    xt = h.reshape(n, D)
    logits = jnp.dot(xt, w_router[i], precision=lax.Precision.HIGHEST)
    aff = jax.nn.softmax(logits, axis=-1)
    gate, idx = lax.top_k(aff.T, cap)
    xe = xt.astype(BF16)[idx]
    ye = _expert_ffn(xe, gate[..., None], w_gate, w_up, w_down, i)
    out = x.reshape(n, D).at[idx.reshape(-1)].add(ye.reshape(-1, D))
    return out.reshape(B, S, D)


def kernel(x_prompt, x_sample, rel_bias, norm_mix, norm_ffn, attn_w_qkv, attn_q_gain, attn_k_gain,
           attn_sink, attn_w_o, rwkv_mu, rwkv_w_rkv, rwkv_w0, rwkv_w1, rwkv_w2, rwkv_a0, rwkv_a1,
           rwkv_a2, rwkv_g1, rwkv_g2, rwkv_k_k, rwkv_k_a, rwkv_r_k, rwkv_ln_w, rwkv_ln_b, rwkv_w_o,
           moe_router, moe_w_gate, moe_w_up, moe_w_down):
    bias = _band_bias(rel_bias)
    assert x_prompt.shape[0] == x_sample.shape[0]
    rwkv_p = [_rwkv_weights(j, rwkv_w_rkv, rwkv_w0, rwkv_w2, rwkv_a0, rwkv_a2, rwkv_g1, rwkv_g2,
                            rwkv_k_k, rwkv_k_a, rwkv_r_k, rwkv_ln_w, rwkv_ln_b, rwkv_w_o,
                            x_prompt.shape[0]) for j in range(DEPTH // 2)]

    def trunk(x):
        for i in range(DEPTH):
            j = i // 2
            h = _rms_norm(x, norm_mix[i])
            if i % 2 == 0:
                x = _attn_layer(x, h, attn_w_qkv, j, attn_q_gain[j], attn_k_gain[j], attn_sink[j],
                                attn_w_o, bias)
            else:
                x = _rwkv_layer(x, h, j, rwkv_mu, rwkv_w1, rwkv_a1, rwkv_p[j])
            x = _moe_layer(x, _rms_norm(x, norm_ffn[i]), i, moe_router, moe_w_gate, moe_w_up,
                           moe_w_down)
        return x

    return (trunk(x_prompt), trunk(x_sample))
```
